```python
import jax
import jax.numpy as jnp
from jax import lax
import numpy as np

D_MODEL = 1024
BATCH = 4
SEQ = 8192
DEPTH = 2

GRID_W = 64
CTX_LEN = 256
D_MIX = D_MODEL
HEAD_DIM = 64
ATTN_HEADS = 8
ATTN_KV_HEADS = 2
ATTN_GROUP = ATTN_HEADS // ATTN_KV_HEADS
ATTN_DIM = ATTN_HEADS * HEAD_DIM
KV_DIM = ATTN_KV_HEADS * HEAD_DIM
Q_BLOCK = 128
ROPE_THETA = 10000.0
AXIS_ROT_DIM = HEAD_DIM // 2
AXIS_FREQS = AXIS_ROT_DIM // 2
RWKV_HEADS = 8
RWKV_HEAD_DIM = 64
RWKV_DIM = RWKV_HEADS * RWKV_HEAD_DIM
DECAY_LORA = 64
ICLR_LORA = 64
GATE_LORA = 128
N_DIRS = 2
ATTN_COLS = ATTN_DIM + 2 * KV_DIM
RWKV_COLS = 3 * RWKV_DIM + N_DIRS * (DECAY_LORA + ICLR_LORA) + GATE_LORA
IN_COLS = ATTN_COLS + RWKV_COLS
SHIFT_WIDTH = 3
N_KEYS = 128
N_EXPERTS = N_KEYS * N_KEYS
PEER_HEADS = 8
PEER_KEY_DIM = 256
PEER_HALF = PEER_KEY_DIM // 2
PEER_TOPK = 16
TOKEN_BLOCK = 128
N_MOD = 6
NORM_EPS = 1e-6
GN_EPS = 64e-5
L2_EPS = 1e-12

kernel_name = 'hybrid_gqa_rwkv7_peer_dit'


def rms_norm(x, gain):
    xf = x.astype(jnp.float32)
    y = xf * lax.rsqrt(jnp.mean(xf * xf, axis=-1, keepdims=True) + NORM_EPS)
    return (y * gain.astype(jnp.float32)).astype(x.dtype)


def modulate(h, shift, scale):
    return h * (1 + scale) + shift


def axial_rope_tables(n_tokens):
    rows = n_tokens // GRID_W
    row = jnp.broadcast_to(jnp.arange(rows, dtype=jnp.float32)[:, None], (rows, GRID_W)).reshape(-1)
    col = jnp.broadcast_to(jnp.arange(GRID_W, dtype=jnp.float32)[None, :], (rows, GRID_W)).reshape(-1)
    inv_freq = ROPE_THETA ** (-jnp.arange(AXIS_FREQS, dtype=jnp.float32) * 2.0 / AXIS_ROT_DIM)
    ang = jnp.stack([row[:, None] * inv_freq, col[:, None] * inv_freq], axis=1)
    return jnp.cos(ang), jnp.sin(ang)


def apply_axial_rope(x, cos, sin):
    B, L, H, _ = x.shape
    xr = x.astype(jnp.float32).reshape(B, L, H, 2, 2, AXIS_FREQS)
    x1, x2 = xr[..., 0, :], xr[..., 1, :]
    c, s = cos[None, :, None], sin[None, :, None]
    out = jnp.stack([x1 * c - x2 * s, x2 * c + x1 * s], axis=-2)
    return out.reshape(B, L, H, HEAD_DIM).astype(x.dtype)


def block_attention(q, k, v):
    B, Lq = q.shape[:2]
    n_blk = Lq // Q_BLOCK
    qb = q.reshape(B, n_blk, Q_BLOCK, ATTN_KV_HEADS, ATTN_GROUP, HEAD_DIM).swapaxes(0, 1)
    scale = HEAD_DIM ** -0.5

    def one_block(q_blk):
        s = jnp.einsum('bqkgd,bskd->bkgqs', q_blk, k).astype(jnp.float32) * scale
        p = jax.nn.softmax(s, axis=-1).astype(v.dtype)
        return jnp.einsum('bkgqs,bskd->bqkgd', p, v)

    o = lax.map(one_block, qb)
    return o.swapaxes(0, 1).reshape(B, Lq, ATTN_DIM)


def centred_shift(x, taps):
    xp = jnp.pad(x, ((0, 0), (1, 1), (0, 0)))
    return xp[:, :-2] * taps[0] + xp[:, 1:-1] * taps[1] + xp[:, 2:] * taps[2]


def rwkv_heads(t):
    return t.reshape(t.shape[0], t.shape[1], RWKV_HEADS, RWKV_HEAD_DIM).astype(jnp.float32)


def rwkv_features(cols, shift_taps, decay_base, decay_up, iclr_base, iclr_up, gate_up, k_k, k_a):
    B, L, _ = cols.shape
    cols = centred_shift(cols, shift_taps)
    o3 = 3 * RWKV_DIM
    o4 = o3 + N_DIRS * DECAY_LORA
    o5 = o4 + N_DIRS * ICLR_LORA
    r, k, v, wd, ad, gd = jnp.split(cols, [RWKV_DIM, 2 * RWKV_DIM, o3, o4, o5], axis=-1)
    wd = jnp.tanh(wd).reshape(B, L, N_DIRS, DECAY_LORA)
    ad = ad.reshape(B, L, N_DIRS, ICLR_LORA)
    z = (decay_base + jnp.einsum('bldr,drc->bldc', wd, decay_up)).astype(jnp.float32)
    decay = jnp.exp(-jnp.exp(-jax.nn.softplus(-z) - 0.5))
    iclr = jax.nn.sigmoid((iclr_base + jnp.einsum('bldr,drc->bldc', ad, iclr_up)).astype(jnp.float32))
    gate = jax.nn.sigmoid(gd) @ gate_up
    kk = rwkv_heads(k * k_k)
    kk = kk * lax.rsqrt(jnp.sum(kk * kk, axis=-1, keepdims=True) + L2_EPS)
    k_dir = k[:, :, None, :].astype(jnp.float32) * (1 + (iclr - 1) * k_a.astype(jnp.float32))
    return r, k, v, gate, decay, iclr, kk, k_dir


def wkv_scan(r, w, k, v, kk, b, s0, reverse, emit):
    xs = tuple(jnp.moveaxis(t, 1, 0) for t in (r, w, k, v, kk, b))

    def step(state, inp):
        r_t, w_t, k_t, v_t, kk_t, b_t = inp
        s_kk = jnp.einsum('bhij,bhj->bhi', state, kk_t)
        state = state * w_t[:, :, None, :] - s_kk[..., None] * b_t[:, :, None, :] + v_t[..., None] * k_t[:, :, None, :]
        y = jnp.einsum('bhij,bhj->bhi', state, r_t) if emit else None
        return state, y

    s_final, ys = lax.scan(step, s0, xs, reverse=reverse)
    return s_final, (jnp.moveaxis(ys, 0, 1) if emit else None)


def rwkv_direction(feats, d, s0, reverse, emit):
    r, k, v, gate, decay, iclr, kk, k_dir = feats
    return wkv_scan(rwkv_heads(r), rwkv_heads(decay[:, :, d]), rwkv_heads(k_dir[:, :, d]), rwkv_heads(v),
                    kk, kk * rwkv_heads(iclr[:, :, d]), s0, reverse, emit)


def rwkv_output(y, feats, r_k, ln_w, ln_b):
    r, k, v, gate, decay, iclr, kk, k_dir = feats
    B, L = y.shape[:2]
    mu = jnp.mean(y, axis=-1, keepdims=True)
    var = jnp.mean(jnp.square(y - mu), axis=-1, keepdims=True)
    yn = ((y - mu) * lax.rsqrt(var + GN_EPS)).reshape(B, L, RWKV_DIM)
    yn = yn * ln_w.astype(jnp.float32) + ln_b.astype(jnp.float32)
    bonus = jnp.sum(rwkv_heads(r) * rwkv_heads(k) * r_k.astype(jnp.float32), axis=-1, keepdims=True) * rwkv_heads(v)
    out = (yn + bonus.reshape(B, L, RWKV_DIM)) * gate.astype(jnp.float32)
    return out.astype(r.dtype)


def rwkv_mixer(cols_lat, cols_ctx, shift_taps, decay_base, decay_up, iclr_base, iclr_up, gate_up,
               k_k, k_a, r_k, ln_w, ln_b, need_ctx_out):
    feats_lat = rwkv_features(cols_lat, shift_taps, decay_base, decay_up, iclr_base, iclr_up, gate_up, k_k, k_a)
    feats_ctx = rwkv_features(cols_ctx, shift_taps, decay_base, decay_up, iclr_base, iclr_up, gate_up, k_k, k_a)
    s0 = jnp.zeros((cols_lat.shape[0], RWKV_HEADS, RWKV_HEAD_DIM, RWKV_HEAD_DIM), jnp.float32)
    ys_lat, ys_ctx = [], []
    for d in range(N_DIRS):
        reverse = d == 1
        s_ctx, y_ctx = rwkv_direction(feats_ctx, d, s0, reverse, need_ctx_out)
        _, y_lat = rwkv_direction(feats_lat, d, s_ctx, reverse, True)
        ys_lat.append(y_lat)
        ys_ctx.append(y_ctx)
    out_lat = rwkv_output(ys_lat[0] + ys_lat[1], feats_lat, r_k, ln_w, ln_b)
    if not need_ctx_out:
        return out_lat, None
    out_ctx = rwkv_output(ys_ctx[0] + ys_ctx[1], feats_ctx, r_k, ln_w, ln_b)
    return out_lat, out_ctx


def token_mixer(h_lat, h_ctx, rope_cos, rope_sin, w_in, q_gain, k_gain, shift_taps, decay_base, decay_up,
                iclr_base, iclr_up, gate_up, k_k, k_a, r_k, ln_w, ln_b, w_out, need_ctx_out):
    p_lat = h_lat @ w_in
    p_ctx = h_ctx @ w_in

    def split_attn(p):
        B, L, _ = p.shape
        q = p[..., :ATTN_DIM].reshape(B, L, ATTN_HEADS, HEAD_DIM)
        k = p[..., ATTN_DIM:ATTN_DIM + KV_DIM].reshape(B, L, ATTN_KV_HEADS, HEAD_DIM)
        v = p[..., ATTN_DIM + KV_DIM:ATTN_COLS].reshape(B, L, ATTN_KV_HEADS, HEAD_DIM)
        return q, rms_norm(k, k_gain), v

    q_l, k_l, v_l = split_attn(p_lat)
    q_c, k_c, v_c = split_attn(p_ctx)
    q_l = apply_axial_rope(rms_norm(q_l, q_gain), rope_cos, rope_sin)
    k_l = apply_axial_rope(k_l, rope_cos, rope_sin)
    attn_lat = block_attention(q_l, jnp.concatenate([k_l, k_c], axis=1), jnp.concatenate([v_l, v_c], axis=1))
    rwkv_lat, rwkv_ctx = rwkv_mixer(p_lat[..., ATTN_COLS:], p_ctx[..., ATTN_COLS:], shift_taps, decay_base,
                                    decay_up, iclr_base, iclr_up, gate_up, k_k, k_a, r_k, ln_w, ln_b,
                                    need_ctx_out)
    out_lat = jnp.concatenate([attn_lat, rwkv_lat], axis=-1) @ w_out
    if not need_ctx_out:
        return out_lat, None
    attn_ctx = block_attention(rms_norm(q_c, q_gain), k_c, v_c)
    out_ctx = jnp.concatenate([attn_ctx, rwkv_ctx], axis=-1) @ w_out
    return out_lat, out_ctx


def peer_ffn(h, w_query, subkeys1, subkeys2, expert_u, expert_v):
    B, L, D = h.shape
    q = (h @ w_query).reshape(B, L, PEER_HEADS, 2, PEER_HALF)
    s1 = jnp.einsum('blhd,kd->blhk', q[..., 0, :], subkeys1).astype(jnp.float32)
    s2 = jnp.einsum('blhd,kd->blhk', q[..., 1, :], subkeys2).astype(jnp.float32)
    v1, i1 = lax.top_k(s1, PEER_TOPK)
    v2, i2 = lax.top_k(s2, PEER_TOPK)
    n_cand = PEER_TOPK * PEER_TOPK
    cand_score = (v1[..., :, None] + v2[..., None, :]).reshape(B, L, PEER_HEADS, n_cand)
    cand_index = (i1[..., :, None] * N_KEYS + i2[..., None, :]).reshape(B, L, PEER_HEADS, n_cand)
    top_score, top_pos = lax.top_k(cand_score, PEER_TOPK)
    expert_idx = jnp.take_along_axis(cand_index, top_pos, axis=-1)
    gates = jax.nn.softmax(top_score, axis=-1)
    n_blk = (B * L) // TOKEN_BLOCK
    hb = h.reshape(n_blk, TOKEN_BLOCK, D)
    ib = expert_idx.reshape(n_blk, TOKEN_BLOCK, PEER_HEADS * PEER_TOPK)
    gb = gates.reshape(n_blk, TOKEN_BLOCK, PEER_HEADS * PEER_TOPK)

    def one_block(args):
        h_blk, i_blk, g_blk = args
        u = jnp.take(expert_u, i_blk, axis=0)
        act = jax.nn.gelu(jnp.einsum('td,ted->te', h_blk, u).astype(jnp.float32), approximate=False)
        v = jnp.take(expert_v, i_blk, axis=0)
        return jnp.einsum('te,ted->td', (g_blk * act).astype(v.dtype), v)

    out = lax.map(one_block, (hb, ib, gb))
    return out.reshape(B, L, D)


def setup_inputs(seed: int = 0) -> dict:
    key = jax.random.key(seed)
    ks = jax.random.split(key, 28)

    def nrm(k, shape, s):
        return jax.random.normal(k, shape, jnp.float32) * s

    return {
        'x': nrm(ks[0], (BATCH, SEQ, D_MODEL), 1.0),
        'c': nrm(ks[1], (BATCH, D_MODEL), 1.0),
        'ctx': nrm(ks[2], (BATCH, CTX_LEN, D_MODEL), 1.0),
        'c_ctx': nrm(ks[3], (D_MODEL,), 1.0),
        'mod_w': nrm(ks[4], (DEPTH, D_MODEL, N_MOD * D_MODEL), 0.5 * D_MODEL ** -0.5),
        'mod_b': nrm(ks[5], (DEPTH, N_MOD * D_MODEL), 0.01),
        'norm_mix': 1.0 + nrm(ks[6], (DEPTH, D_MODEL), 0.02),
        'norm_ffn': 1.0 + nrm(ks[7], (DEPTH, D_MODEL), 0.02),
        'w_in': nrm(ks[8], (DEPTH, D_MODEL, IN_COLS), D_MODEL ** -0.5),
        'q_gain': 1.0 + nrm(ks[9], (DEPTH, HEAD_DIM), 0.02),
        'k_gain': 1.0 + nrm(ks[10], (DEPTH, HEAD_DIM), 0.02),
        'shift_taps': jnp.array([0.25, 1.0, 0.25], jnp.float32)[None, :, None]
                      + nrm(ks[11], (DEPTH, SHIFT_WIDTH, RWKV_COLS), 0.05),
        'decay_base': jax.random.uniform(ks[12], (DEPTH, N_DIRS, RWKV_DIM), jnp.float32, -6.0, 2.0),
        'decay_up': nrm(ks[13], (DEPTH, N_DIRS, DECAY_LORA, RWKV_DIM), 0.1),
        'iclr_base': nrm(ks[14], (DEPTH, N_DIRS, RWKV_DIM), 0.5),
        'iclr_up': nrm(ks[15], (DEPTH, N_DIRS, ICLR_LORA, RWKV_DIM), 0.1),
        'gate_up': nrm(ks[16], (DEPTH, GATE_LORA, RWKV_DIM), GATE_LORA ** -0.5),
        'k_k': 0.85 + nrm(ks[17], (DEPTH, RWKV_DIM), 0.05),
        'k_a': 1.0 + nrm(ks[18], (DEPTH, RWKV_DIM), 0.05),
        'r_k': nrm(ks[19], (DEPTH, RWKV_HEADS, RWKV_HEAD_DIM), 0.1),
        'ln_x_w': 1.0 + nrm(ks[20], (DEPTH, RWKV_DIM), 0.02),
        'ln_x_b': nrm(ks[21], (DEPTH, RWKV_DIM), 0.02),
        'w_out': nrm(ks[22], (DEPTH, D_MIX, D_MODEL), D_MIX ** -0.5),
        'peer_query': nrm(ks[23], (DEPTH, D_MODEL, PEER_HEADS * PEER_KEY_DIM), D_MODEL ** -0.5),
        'peer_subkeys1': nrm(ks[24], (DEPTH, N_KEYS, PEER_HALF), PEER_HALF ** -0.5),
        'peer_subkeys2': nrm(ks[25], (DEPTH, N_KEYS, PEER_HALF), PEER_HALF ** -0.5),
        'expert_u': nrm(ks[26], (DEPTH, N_EXPERTS, D_MODEL), D_MODEL ** -0.5),
        'expert_v': nrm(ks[27], (DEPTH, N_EXPERTS, D_MODEL), 0.3),
    }


def reference(x, c, ctx, c_ctx, mod_w, mod_b, norm_mix, norm_ffn, w_in, q_gain, k_gain, shift_taps,
              decay_base, decay_up, iclr_base, iclr_up, gate_up, k_k, k_a, r_k, ln_x_w, ln_x_b, w_out,
              peer_query, peer_subkeys1, peer_subkeys2, expert_u, expert_v):
    rope_cos, rope_sin = axial_rope_tables(x.shape[1])
    for l in range(DEPTH):
        need_ctx = l < DEPTH - 1
        mod_lat = jax.nn.silu(c) @ mod_w[l] + mod_b[l]
        mod_ctx = jax.nn.silu(c_ctx) @ mod_w[l] + mod_b[l]
        sh_m, sc_m, g_m, sh_f, sc_f, g_f = jnp.split(mod_lat[:, None, :], N_MOD, axis=-1)
        csh_m, csc_m, cg_m, csh_f, csc_f, cg_f = jnp.split(mod_ctx, N_MOD, axis=-1)
        h_lat = modulate(rms_norm(x, norm_mix[l]), sh_m, sc_m)
        h_ctx = modulate(rms_norm(ctx, norm_mix[l]), csh_m, csc_m)
        mix_lat, mix_ctx = token_mixer(h_lat, h_ctx, rope_cos, rope_sin, w_in[l], q_gain[l], k_gain[l],
                                       shift_taps[l], decay_base[l], decay_up[l], iclr_base[l], iclr_up[l],
                                       gate_up[l], k_k[l], k_a[l], r_k[l], ln_x_w[l], ln_x_b[l], w_out[l],
                                       need_ctx)
        x = x + g_m * mix_lat
        x = x + g_f * peer_ffn(modulate(rms_norm(x, norm_ffn[l]), sh_f, sc_f), peer_query[l],
                               peer_subkeys1[l], peer_subkeys2[l], expert_u[l], expert_v[l])
        if need_ctx:
            ctx = ctx + cg_m * mix_ctx
            ctx = ctx + cg_f * peer_ffn(modulate(rms_norm(ctx, norm_ffn[l]), csh_f, csc_f), peer_query[l],
                                        peer_subkeys1[l], peer_subkeys2[l], expert_u[l], expert_v[l])
    return x
```

```python
import functools
import math

import jax
import jax.numpy as jnp
from jax import lax
from jax.experimental import pallas as pl
from jax.experimental.pallas import tpu as pltpu

F32, BF16 = jnp.float32, jnp.bfloat16
HIGHEST = lax.Precision.HIGHEST

HEAD_DIM = 64
ATTN_HEADS = 8
ATTN_KV_HEADS = 2
ATTN_GROUP = ATTN_HEADS // ATTN_KV_HEADS
ATTN_DIM = ATTN_HEADS * HEAD_DIM
KV_DIM = ATTN_KV_HEADS * HEAD_DIM
ATTN_COLS = ATTN_DIM + 2 * KV_DIM
GRID_W = 64
ROPE_THETA = 10000.0
AXIS_ROT_DIM = HEAD_DIM // 2
AXIS_FREQS = AXIS_ROT_DIM // 2
RWKV_DIM = 512
LORA_COLS = 128
N_DIRS = 2
RWKV_COLS = 3 * RWKV_DIM + 3 * LORA_COLS
N_KEYS = 128
PEER_HEADS = 8
PEER_HALF = 128
PEER_TOPK = 16
N_MOD = 6
NORM_EPS = 1e-6
GN_EPS = 64e-5
L2_EPS = 1e-12

LANES = 128
TOKEN_TILE = 256
KV_TILE = 768
CHUNK = 64
EXPERT_TILE = 256
VMEM_LIMIT = 48 * 1024 * 1024


def _params(sem):
    return pltpu.CompilerParams(dimension_semantics=sem, vmem_limit_bytes=VMEM_LIMIT)


def _iota(shape, axis):
    return lax.broadcasted_iota(jnp.int32, shape, axis)


def _head_ones(n):
    return ((_iota((n, n), 0) >> 6) == (_iota((n, n), 1) >> 6)).astype(BF16)


def _head_sum(x, ones):
    n = ones.shape[0]
    outs = []
    for c0 in range(0, x.shape[1], n):
        rest = x[:, c0:c0 + n]
        acc = None
        for _ in range(3):
            part = rest.astype(BF16)
            rest = rest - part.astype(F32)
            t = jnp.dot(part, ones, preferred_element_type=F32)
            acc = t if acc is None else acc + t
        outs.append(acc)
    return outs[0] if len(outs) == 1 else jnp.concatenate(outs, axis=1)


def _norm_mod(x, gain, shift, scale):
    ms = jnp.mean(x * x, axis=-1, keepdims=True)
    return (x * lax.rsqrt(ms + NORM_EPS) * gain) * (1.0 + scale) + shift


def _mod_kernel(c_ref, w_ref, b_ref, o_ref):
    c = c_ref[...]
    a = c * jax.nn.sigmoid(c)
    o_ref[...] = jnp.dot(a, w_ref[...], preferred_element_type=F32, precision=HIGHEST) + b_ref[...]


def _modulation(cc, mod_w, mod_b):
    depth, d, n = mod_w.shape
    return pl.pallas_call(
        _mod_kernel,
        grid=(depth, n // d),
        in_specs=[pl.BlockSpec((8, d), lambda l, j: (0, 0)),
                  pl.BlockSpec((None, d, d), lambda l, j: (l, 0, j)),
                  pl.BlockSpec((None, 1, d), lambda l, j: (l, 0, j))],
        out_specs=pl.BlockSpec((None, 8, d), lambda l, j: (l, 0, j)),
        out_shape=jax.ShapeDtypeStruct((depth, 8, n), F32),
        compiler_params=_params(("parallel", "parallel")),
        name="modulation",
    )(cc, mod_w, mod_b.reshape(depth, 1, n))


def _mod_spec(batch, chunk, d):
    return pl.BlockSpec((None, 1, d), lambda b, i, *_: (jnp.where(i == 0, batch, b), 0, chunk))


def _rope(x, cos, sin):
    w = x.shape[1]
    reps = w // LANES
    if reps > 1:
        cos = jnp.concatenate([cos] * reps, axis=1)
        sin = jnp.concatenate([sin] * reps, axis=1)
    first = (_iota(x.shape, 1) & AXIS_FREQS) == 0
    partner = jnp.where(first, pltpu.roll(x, w - AXIS_FREQS, 1), pltpu.roll(x, AXIS_FREQS, 1))
    return x * cos + partner * sin


def _head_norm(x, gain, ones):
    ms = _head_sum(x * x, ones) * (1.0 / HEAD_DIM)
    return x * lax.rsqrt(ms + NORM_EPS) * gain


def _in_kernel(x_ref, gain_ref, sh_ref, sc_ref, w_ref, qg_ref, kg_ref, cos_ref, sin_ref,
               q_ref, k_ref, v_ref, rw_ref):
    h = _norm_mod(x_ref[...], gain_ref[...], sh_ref[...], sc_ref[...])
    p = jnp.dot(h.astype(BF16), w_ref[...], preferred_element_type=F32)
    cos, sin = cos_ref[...], sin_ref[...]
    q = _head_norm(p[:, :ATTN_DIM], qg_ref[...], _head_ones(256))
    q_ref[...] = (_rope(q, cos, sin) * (HEAD_DIM ** -0.5)).astype(BF16)
    k = _head_norm(p[:, ATTN_DIM:ATTN_DIM + KV_DIM], kg_ref[...], _head_ones(KV_DIM))
    k_ref[...] = _rope(k, cos, sin).astype(BF16)
    v_ref[...] = p[:, ATTN_DIM + KV_DIM:ATTN_COLS].astype(BF16)
    rw_ref[...] = p[:, ATTN_COLS:]


def _in_proj(xs, mod, gain, w_in, q_gain, k_gain, cos_t, sin_t):
    b, t, d = xs.shape
    tm = TOKEN_TILE
    n = w_in.shape[1]
    tile = lambda w: pl.BlockSpec((None, tm, w), lambda bb, i: (bb, i, 0))
    row = lambda w: pl.BlockSpec((1, w), lambda bb, i: (0, 0))
    return pl.pallas_call(
        _in_kernel,
        grid=(b, t // tm),
        in_specs=[tile(d), row(d), _mod_spec(b, 0, d), _mod_spec(b, 1, d),
                  pl.BlockSpec((d, n), lambda bb, i: (0, 0)), row(ATTN_DIM), row(KV_DIM),
                  pl.BlockSpec((tm, LANES), lambda bb, i: (i, 0)),
                  pl.BlockSpec((tm, LANES), lambda bb, i: (i, 0))],
        out_specs=[tile(ATTN_DIM), tile(KV_DIM), tile(KV_DIM), tile(RWKV_COLS)],
        out_shape=[jax.ShapeDtypeStruct((b, t, ATTN_DIM), BF16),
                   jax.ShapeDtypeStruct((b, t, KV_DIM), BF16),
                   jax.ShapeDtypeStruct((b, t, KV_DIM), BF16),
                   jax.ShapeDtypeStruct((b, t, RWKV_COLS), F32)],
        compiler_params=_params(("parallel", "parallel")),
        name="in_proj",
    )(xs, gain, mod, mod, w_in, q_gain, k_gain, cos_t, sin_t)


def _attn_step(qs_ref, k_ref, v_ref, m_ref, l_ref, acc_ref, n_valid):
    for g in range(ATTN_KV_HEADS):
        k = k_ref[:, g * HEAD_DIM:(g + 1) * HEAD_DIM]
        v = v_ref[:, g * HEAD_DIM:(g + 1) * HEAD_DIM]
        s = lax.dot_general(qs_ref[g], k, (((1,), (1,)), ((), ())), preferred_element_type=F32)
        if n_valid is not None:
            s = jnp.where(_iota(s.shape, 1) < n_valid, s, -jnp.inf)
        m_old = m_ref[g]
        m_new = jnp.maximum(m_old, jnp.max(s, axis=1, keepdims=True))
        alpha = jnp.exp(m_old - m_new)
        p = jnp.exp(s - m_new)
        l_ref[g] = alpha * l_ref[g] + jnp.sum(p, axis=1, keepdims=True)
        acc_ref[g] = alpha * acc_ref[g] + jnp.dot(p.astype(BF16), v, preferred_element_type=F32)
        m_ref[g] = m_new


def _attn_kernel(q_ref, k_ref, v_ref, o_ref, qs_ref, m_ref, l_ref, acc_ref, *, n_ctx, nk):
    qi, kj = pl.program_id(1), pl.program_id(2)
    tq = q_ref.shape[0]

    @pl.when(kj == 0)
    def _():
        for hh in range(ATTN_HEADS):
            g, h = divmod(hh, ATTN_GROUP)
            qs_ref[g, h * tq:(h + 1) * tq, :] = q_ref[:, hh * HEAD_DIM:(hh + 1) * HEAD_DIM]
        m_ref[...] = jnp.full(m_ref.shape, -jnp.inf, F32)
        l_ref[...] = jnp.zeros(l_ref.shape, F32)
        acc_ref[...] = jnp.zeros(acc_ref.shape, F32)

    @pl.when((qi == 0) & (kj == 0))
    def _():
        _attn_step(qs_ref, k_ref, v_ref, m_ref, l_ref, acc_ref, n_ctx)

    @pl.when(qi > 0)
    def _():
        _attn_step(qs_ref, k_ref, v_ref, m_ref, l_ref, acc_ref, None)

    @pl.when(kj == nk - 1)
    def _():
        for hh in range(ATTN_HEADS):
            g, h = divmod(hh, ATTN_GROUP)
            o = acc_ref[g, h * tq:(h + 1) * tq, :] / l_ref[g, h * tq:(h + 1) * tq, :]
            o_ref[:, hh * HEAD_DIM:(hh + 1) * HEAD_DIM] = o.astype(BF16)


def _attention(q, k, v, n_ctx):
    b, t, _ = q.shape
    tq = TOKEN_TILE
    tk = KV_TILE if t % KV_TILE == 0 else TOKEN_TILE
    nk = t // tk
    assert n_ctx == tq and n_ctx <= tk
    kv_spec = pl.BlockSpec((None, tk, KV_DIM), lambda bb, i, j: (bb, jnp.where(i == 0, 0, j), 0))
    return pl.pallas_call(
        functools.partial(_attn_kernel, n_ctx=n_ctx, nk=nk),
        grid=(b, t // tq, nk),
        in_specs=[pl.BlockSpec((None, tq, ATTN_DIM), lambda bb, i, j: (bb, i, 0)), kv_spec, kv_spec],
        out_specs=pl.BlockSpec((None, tq, ATTN_DIM), lambda bb, i, j: (bb, i, 0)),
        out_shape=jax.ShapeDtypeStruct((b, t, ATTN_DIM), BF16),
        scratch_shapes=[pltpu.VMEM((ATTN_KV_HEADS, ATTN_GROUP * tq, HEAD_DIM), BF16),
                        pltpu.VMEM((ATTN_KV_HEADS, ATTN_GROUP * tq, 1), F32),
                        pltpu.VMEM((ATTN_KV_HEADS, ATTN_GROUP * tq, 1), F32),
                        pltpu.VMEM((ATTN_KV_HEADS, ATTN_GROUP * tq, HEAD_DIM), F32)],
        compiler_params=_params(("parallel", "parallel", "arbitrary")),
        name="attention",
    )(q, k, v)


def _feat_kernel(rw_ref, prev_ref, next_ref, taps_ref, dbase_ref, dup_ref, ibase_ref, iup_ref, gup_ref,
                 kk_par_ref, ka_ref, rk_ref,
                 r_out, v_out, kk_out, gate_out, bonus_out, lw_out, kd_out, bb_out, *, nt):
    i = pl.program_id(1)
    x = rw_ref[...]
    tm = x.shape[0]
    prev_row = jnp.where(i <= 1, 0.0, prev_ref[7:8, :])
    next_row = jnp.where((i == 0) | (i == nt - 1), 0.0, next_ref[0:1, :])
    row = _iota(x.shape, 0)
    xm = jnp.where(row == 0, prev_row, pltpu.roll(x, 1, 0))
    xp = jnp.where(row == tm - 1, next_row, pltpu.roll(x, tm - 1, 0))
    xs = xm * taps_ref[0:1, :] + x * taps_ref[1:2, :] + xp * taps_ref[2:3, :]

    o3 = 3 * RWKV_DIM
    r, k, v = xs[:, :RWKV_DIM], xs[:, RWKV_DIM:2 * RWKV_DIM], xs[:, 2 * RWKV_DIM:o3]
    wd = jnp.tanh(xs[:, o3:o3 + LORA_COLS]).astype(BF16)
    ad = xs[:, o3 + LORA_COLS:o3 + 2 * LORA_COLS].astype(BF16)
    gd = jax.nn.sigmoid(xs[:, o3 + 2 * LORA_COLS:]).astype(BF16)
    ones = _head_ones(256)
    kkr = k * kk_par_ref[...]
    kk = kkr * lax.rsqrt(_head_sum(kkr * kkr, ones) + L2_EPS)
    r_out[...] = r
    v_out[...] = v
    kk_out[...] = kk
    gate_out[...] = jnp.dot(gd, gup_ref[...], preferred_element_type=F32)
    bonus_out[...] = _head_sum(r * k * rk_ref[...], ones) * v
    for d in range(N_DIRS):
        z = dbase_ref[d:d + 1, :] + jnp.dot(wd, dup_ref[d], preferred_element_type=F32)
        lw_out[d] = jax.nn.sigmoid(z) * (-math.exp(-0.5))
        iclr = jax.nn.sigmoid(ibase_ref[d:d + 1, :] + jnp.dot(ad, iup_ref[d], preferred_element_type=F32))
        kd_out[d] = k * (1.0 + (iclr - 1.0) * ka_ref[...])
        bb_out[d] = kk * iclr


def _rwkv_features(rw, taps, dbase, dup, ibase, iup, gup, k_k, k_a, r_k):
    b, t, w = rw.shape
    tm = TOKEN_TILE
    nt = t // tm
    r8 = tm // 8
    full = lambda a: pl.BlockSpec(a.shape, lambda bb, i: (0,) * a.ndim)
    tile = pl.BlockSpec((None, tm, RWKV_DIM), lambda bb, i: (bb, i, 0))
    dtile = pl.BlockSpec((None, N_DIRS, tm, RWKV_DIM), lambda bb, i: (bb, 0, i, 0))
    shared = jax.ShapeDtypeStruct((b, t, RWKV_DIM), F32)
    perdir = jax.ShapeDtypeStruct((b, N_DIRS, t, RWKV_DIM), F32)
    consts = (taps, dbase, dup, ibase, iup, gup, k_k, k_a, r_k)
    return pl.pallas_call(
        functools.partial(_feat_kernel, nt=nt),
        grid=(b, nt),
        in_specs=[pl.BlockSpec((None, tm, w), lambda bb, i: (bb, i, 0)),
                  pl.BlockSpec((None, 8, w), lambda bb, i: (bb, jnp.maximum(i * r8 - 1, 0), 0)),
                  pl.BlockSpec((None, 8, w), lambda bb, i: (bb, jnp.minimum((i + 1) * r8, nt * r8 - 1), 0))]
                 + [full(a) for a in consts],
        out_specs=[tile] * 5 + [dtile] * 3,
        out_shape=[shared] * 5 + [perdir] * 3,
        compiler_params=_params(("parallel", "parallel")),
        name="rwkv_features",
    )(rw, rw, rw, *consts)


def _scan_kernel(r_ref, v_ref, kk_ref, lw_ref, kd_ref, bb_ref, y_ref, st_ref):
    rev = pl.program_id(1) == 1
    c = pl.program_id(2)

    @pl.when(c == 0)
    def _():
        st_ref[...] = jnp.zeros(st_ref.shape, F32)

    ch, pw = CHUNK, LANES
    ti, si = _iota((ch, ch), 0), _iota((ch, ch), 1)
    tri = (jnp.where(rev, si - ti, ti - si) >= 0).astype(F32)
    a, bcol = _iota((pw, pw), 0), _iota((pw, pw), 1)
    same = (a >> 6) == (bcol >> 6)
    ahead = jnp.where(rev, a - bcol, bcol - a)
    strict = same & (ahead > 0)
    incl = same & (ahead >= 0)
    eye = a == bcol
    head0 = _iota((ch, pw), 1) < HEAD_DIM

    def stack(x):
        return jnp.concatenate([jnp.where(head0, x, 0.0), jnp.where(head0, 0.0, x)], axis=0)

    def dup(x):
        return jnp.concatenate([x, x], axis=0)

    def mm(x, y):
        return jnp.dot(x.astype(BF16), y.astype(BF16), preferred_element_type=F32)

    for p in range(RWKV_DIM // pw):
        sl = slice(p * pw, (p + 1) * pw)
        lw, r, v, kk = lw_ref[:, sl], r_ref[:, sl], v_ref[:, sl], kk_ref[:, sl]
        kd, bb = kd_ref[:, sl], bb_ref[:, sl]
        cum = jnp.dot(tri, lw, preferred_element_type=F32, precision=HIGHEST)
        tot = jnp.sum(lw, axis=0, keepdims=True)
        w_inv, w_rem = jnp.exp(-cum), jnp.exp(tot - cum)
        at, rt = -kk * jnp.exp(cum - lw), r * jnp.exp(cum)
        bt, kt, bw, kw = bb * w_inv, kd * w_inv, bb * w_rem, kd * w_rem

        lhs = jnp.concatenate([stack(bt), stack(kt)], axis=0).astype(BF16)
        rhs = jnp.concatenate([dup(at), dup(rt)], axis=0).astype(BF16)
        m = lax.dot_general(lhs, rhs, (((1,), (1,)), ((), ())), preferred_element_type=F32)
        mab = jnp.where(strict, m[:pw, :pw], 0.0)
        mbr = jnp.where(incl, m[:pw, pw:], 0.0)
        mak = jnp.where(strict, m[pw:, :pw], 0.0)
        mkr = jnp.where(incl, m[pw:, pw:], 0.0)

        n, x = mab, jnp.concatenate([stack(bw), mbr], axis=1)
        levels = CHUNK.bit_length() - 1
        for lvl in range(levels):
            if lvl < levels - 1:
                nx = mm(n, jnp.concatenate([n, x], axis=1))
                n, x = nx[:, :pw], x + nx[:, pw:]
            else:
                x = x + mm(n, x)
        z, tm_ = x[:, :pw], x[:, pw:]

        a2, v2 = stack(at), stack(v)
        pt = mm(z.T, a2) + jnp.where(eye, jnp.exp(tot), 0.0)
        qt = mm((mm(mak, z) + stack(kw)).T, v2)
        g = stack(rt) + mm(tm_.T, a2)
        hm = mkr + mm(mak, tm_)
        st = st_ref[p]
        ys = mm(g, st) + mm(hm.T, v2)
        y_ref[:, sl] = ys[:ch] + ys[ch:]
        st_ref[p] = mm(pt, st) + qt


def _rwkv_scan(r, v, kk, lw, kd, bb, n_ctx):
    b, t, w = r.shape
    nc, ncx = t // CHUNK, n_ctx // CHUNK

    def chunk(d, c):
        return jnp.where(d == 0, c, jnp.where(c < ncx, ncx - 1 - c, nc - 1 + ncx - c))

    shared = pl.BlockSpec((None, CHUNK, w), lambda bb_, d, c: (bb_, chunk(d, c), 0))
    perdir = pl.BlockSpec((None, None, CHUNK, w), lambda bb_, d, c: (bb_, d, chunk(d, c), 0))
    return pl.pallas_call(
        _scan_kernel,
        grid=(b, N_DIRS, nc),
        in_specs=[shared] * 3 + [perdir] * 3,
        out_specs=perdir,
        out_shape=jax.ShapeDtypeStruct((b, N_DIRS, t, w), F32),
        scratch_shapes=[pltpu.VMEM((w // LANES, LANES, LANES), F32)],
        compiler_params=_params(("parallel", "parallel", "arbitrary")),
        name="rwkv_scan",
    )(r, v, kk, lw, kd, bb)


def _out_kernel(y0_ref, y1_ref, bonus_ref, gate_ref, attn_ref, x_ref, lnw_ref, lnb_ref, w_ref, g_ref, o_ref):
    ones = _head_ones(256)
    y = y0_ref[...] + y1_ref[...]
    yc = y - _head_sum(y, ones) * (1.0 / HEAD_DIM)
    var = _head_sum(yc * yc, ones) * (1.0 / HEAD_DIM)
    yn = yc * lax.rsqrt(var + GN_EPS) * lnw_ref[...] + lnb_ref[...]
    rw = ((yn + bonus_ref[...]) * gate_ref[...]).astype(BF16)
    mix = jnp.dot(jnp.concatenate([attn_ref[...], rw], axis=1), w_ref[...], preferred_element_type=F32)
    o_ref[...] = x_ref[...] + g_ref[...] * mix


def _mixer_out(y, bonus, gate, attn, xs, ln_w, ln_b, w_out, mod):
    b, t, d = xs.shape
    tm = TOKEN_TILE
    tile = lambda w: pl.BlockSpec((None, tm, w), lambda bb, i: (bb, i, 0))
    ydir = lambda dd: pl.BlockSpec((None, None, tm, RWKV_DIM), lambda bb, i: (bb, dd, i, 0))
    row = lambda w: pl.BlockSpec((1, w), lambda bb, i: (0, 0))
    return pl.pallas_call(
        _out_kernel,
        grid=(b, t // tm),
        in_specs=[ydir(0), ydir(1), tile(RWKV_DIM), tile(RWKV_DIM), tile(ATTN_DIM), tile(d),
                  row(RWKV_DIM), row(RWKV_DIM), pl.BlockSpec(w_out.shape, lambda bb, i: (0, 0)),
                  _mod_spec(b, 2, d)],
        out_specs=tile(d),
        out_shape=jax.ShapeDtypeStruct((b, t, d), F32),
        compiler_params=_params(("parallel", "parallel")),
        name="mixer_out",
    )(y, y, bonus, gate, attn, xs, ln_w, ln_b, w_out, mod)


def _query_kernel(x_ref, gain_ref, sh_ref, sc_ref, w_ref, k1_ref, k2_ref, h_ref, s1_ref, s2_ref):
    h = _norm_mod(x_ref[...], gain_ref[...], sh_ref[...], sc_ref[...]).astype(BF16)
    h_ref[...] = h
    q = jnp.dot(h, w_ref[...], preferred_element_type=F32).astype(BF16)
    nt = (((1,), (1,)), ((), ()))
    for hd in range(PEER_HEADS):
        c0 = hd * 2 * PEER_HALF
        s1_ref[hd] = lax.dot_general(k1_ref[...], q[:, c0:c0 + PEER_HALF], nt, preferred_element_type=F32)
        s2_ref[hd] = lax.dot_general(k2_ref[...], q[:, c0 + PEER_HALF:c0 + 2 * PEER_HALF], nt,
                                     preferred_element_type=F32)


def _peer_scores(xs, mod, gain, w_query, keys1, keys2):
    b, t, d = xs.shape
    tm = TOKEN_TILE
    nt = t // tm
    row = lambda w: pl.BlockSpec((1, w), lambda bb, i: (0, 0))
    full = lambda a: pl.BlockSpec(a.shape, lambda bb, i: (0,) * a.ndim)
    sspec = pl.BlockSpec((PEER_HEADS, N_KEYS, tm), lambda bb, i: (0, 0, bb * nt + i))
    sshape = jax.ShapeDtypeStruct((PEER_HEADS, N_KEYS, b * t), F32)
    return pl.pallas_call(
        _query_kernel,
        grid=(b, nt),
        in_specs=[pl.BlockSpec((None, tm, d), lambda bb, i: (bb, i, 0)), row(d), _mod_spec(b, 3, d),
                  _mod_spec(b, 4, d), full(w_query), full(keys1), full(keys2)],
        out_specs=[pl.BlockSpec((None, tm, d), lambda bb, i: (bb, i, 0)), sspec, sspec],
        out_shape=[jax.ShapeDtypeStruct((b, t, d), BF16), sshape, sshape],
        compiler_params=_params(("parallel", "parallel")),
        name="peer_scores",
    )(xs, gain, mod, mod, w_query, keys1, keys2)


def _top_rows(x_ref, out_ref):
    n = x_ref.shape[0]
    idx = _iota(x_ref.shape, 0).astype(F32)

    def body(i, carry):
        x = x_ref[...]
        m = jnp.max(x, axis=0, keepdims=True)
        out_ref[pl.ds(i, 1), :] = m
        first = jnp.min(jnp.where(x == m, idx, float(n)), axis=0, keepdims=True)
        x_ref[...] = jnp.where(idx == first, -jnp.inf, x)
        return carry

    lax.fori_loop(0, PEER_TOPK, body, 0)


def _topk_kernel(s1_ref, s2_ref, thr_ref, lse_ref, work_ref, v1_ref, v2_ref, cand_ref, top_ref):
    def head(hd, carry):
        work_ref[...] = s1_ref[hd]
        _top_rows(work_ref, v1_ref)
        work_ref[...] = s2_ref[hd]
        _top_rows(work_ref, v2_ref)
        v2 = v2_ref[...]
        for i in range(PEER_TOPK):
            cand_ref[i * PEER_TOPK:(i + 1) * PEER_TOPK, :] = v1_ref[i:i + 1, :] + v2
        _top_rows(cand_ref, top_ref)
        top = top_ref[...]
        mx = top[0:1, :]
        thr_ref[pl.ds(hd, 1), :] = top[PEER_TOPK - 1:PEER_TOPK, :]
        lse_ref[pl.ds(hd, 1), :] = mx + jnp.log(jnp.sum(jnp.exp(top - mx), axis=0, keepdims=True))
        return carry

    lax.fori_loop(0, PEER_HEADS, head, 0)


def _peer_topk(s1, s2):
    _, _, n = s1.shape
    tl = TOKEN_TILE
    sspec = pl.BlockSpec((PEER_HEADS, N_KEYS, tl), lambda i: (0, 0, i))
    ospec = pl.BlockSpec((PEER_HEADS, tl), lambda i: (0, i))
    oshape = jax.ShapeDtypeStruct((PEER_HEADS, n), F32)
    return pl.pallas_call(
        _topk_kernel,
        grid=(n // tl,),
        in_specs=[sspec, sspec],
        out_specs=[ospec, ospec],
        out_shape=[oshape, oshape],
        scratch_shapes=[pltpu.VMEM((N_KEYS, tl), F32), pltpu.VMEM((PEER_TOPK, tl), F32),
                        pltpu.VMEM((PEER_TOPK, tl), F32), pltpu.VMEM((PEER_TOPK * PEER_TOPK, tl), F32),
                        pltpu.VMEM((PEER_TOPK, tl), F32)],
        compiler_params=_params(("parallel",)),
        name="peer_topk",
    )(s1, s2)


def _peer_kernel(h_ref, s1_ref, s2_ref, thr_ref, lse_ref, u_ref, vt_ref, x_ref, g_ref, o_ref, acc_ref, *, n_steps):
    ap = pl.program_id(2)

    @pl.when(ap == 0)
    def _():
        acc_ref[...] = jnp.zeros(acc_ref.shape, F32)

    pre =lax.dot_general(u_ref[...], h_ref[...], (((1,), (1,)), ((), ())), preferred_element_type=F32)
    act = 0.5 * pre * (1.0 + lax.erf(pre * (2.0 ** -0.5)))
    gates = []
    for sub in range(EXPERT_TILE // N_KEYS):
        a = ap * (EXPERT_TILE // N_KEYS) + sub
        w = None
        for hd in range(PEER_HEADS):
            pair = s1_ref[hd, pl.ds(a, 1), :] + s2_ref[hd]
            e = jnp.where(pair >= thr_ref[hd:hd + 1, :], jnp.exp(pair - lse_ref[hd:hd + 1, :]), 0.0)
            w = e if w is None else w + e
        gates.append(w)
    gw = (jnp.concatenate(gates, axis=0) * act).astype(BF16)
    acc_ref[...] += jnp.dot(vt_ref[...], gw, preferred_element_type=F32)

    @pl.when(ap == n_steps - 1)
    def _():
        o_ref[...] = x_ref[...] + g_ref[...] * acc_ref[...].T


def _peer_experts(hq, s1, s2, thr, lse, u, vt, xs, mod):
    b, t, d = xs.shape
    tm = TOKEN_TILE
    nt = t // tm
    n_steps = u.shape[0] // EXPERT_TILE
    tok = lambda bb, i, a: bb * nt + i
    sspec = pl.BlockSpec((PEER_HEADS, N_KEYS, tm), lambda bb, i, a: (0, 0, tok(bb, i, a)))
    rspec = pl.BlockSpec((PEER_HEADS, tm), lambda bb, i, a: (0, tok(bb, i, a)))
    tile = pl.BlockSpec((None, tm, d), lambda bb, i, a: (bb, i, 0))
    return pl.pallas_call(
        functools.partial(_peer_kernel, n_steps=n_steps),
        grid=(b, nt, n_steps),
        in_specs=[tile, sspec, sspec, rspec, rspec,
                  pl.BlockSpec((EXPERT_TILE, d), lambda bb, i, a: (a, 0)),
                  pl.BlockSpec((d, EXPERT_TILE), lambda bb, i, a: (0, a)),
                  tile, _mod_spec(b, 5, d)],
        out_specs=tile,
        out_shape=jax.ShapeDtypeStruct((b, t, d), F32),
        scratch_shapes=[pltpu.VMEM((d, tm), F32)],
        compiler_params=_params(("parallel", "parallel", "arbitrary")),
        name="peer_experts",
    )(hq, s1, s2, thr, lse, u, vt, xs, mod)


def _rope_tables(n_lat, n_ctx):
    pos = jnp.arange(n_lat)
    inv_freq = ROPE_THETA ** (-jnp.arange(AXIS_FREQS, dtype=F32) * 2.0 / AXIS_ROT_DIM)
    ang = jnp.stack([(pos // GRID_W).astype(F32)[:, None] * inv_freq,
                     (pos % GRID_W).astype(F32)[:, None] * inv_freq], axis=1)
    cos = jnp.broadcast_to(jnp.cos(ang)[:, :, None, :], (n_lat, 2, 2, AXIS_FREQS)).reshape(n_lat, HEAD_DIM)
    sin = jnp.stack([-jnp.sin(ang), jnp.sin(ang)], axis=2).reshape(n_lat, HEAD_DIM)
    cos = jnp.concatenate([jnp.ones((n_ctx, HEAD_DIM), F32), cos], axis=0)
    sin = jnp.concatenate([jnp.zeros((n_ctx, HEAD_DIM), F32), sin], axis=0)
    return jnp.tile(cos, (1, LANES // HEAD_DIM)), jnp.tile(sin, (1, LANES // HEAD_DIM))


def _pad_lora(up):
    z = jnp.zeros_like(up[0])
    return jnp.stack([jnp.concatenate([up[0], z], axis=0), jnp.concatenate([z, up[1]], axis=0)]).astype(BF16)


def kernel(x, c, ctx, c_ctx, mod_w, mod_b, norm_mix, norm_ffn, w_in, q_gain, k_gain, shift_taps, decay_base,
           decay_up, iclr_base, iclr_up, gate_up, k_k, k_a, r_k, ln_x_w, ln_x_b, w_out, peer_query, peer_subkeys1,
           peer_subkeys2, expert_u, expert_v):
    b, n_lat, d = x.shape
    n_ctx = ctx.shape[1]
    depth = mod_w.shape[0]
    assert n_ctx == TOKEN_TILE and n_lat % TOKEN_TILE == 0 and b < 8

    xs = jnp.concatenate([ctx, x], axis=1)
    cc = jnp.concatenate([c, c_ctx[None, :], jnp.zeros((8 - b - 1, d), F32)], axis=0)
    mod_all = _modulation(cc, mod_w, mod_b)
    cos_t, sin_t = _rope_tables(n_lat, n_ctx)
    row = lambda a: a.reshape(1, -1)

    for l in range(depth):
        mod = mod_all[l].reshape(8, 1, N_MOD * d)
        q, k, v, rw = _in_proj(xs, mod, row(norm_mix[l]), w_in[l].astype(BF16),
                               row(jnp.tile(q_gain[l], ATTN_HEADS)), row(jnp.tile(k_gain[l], ATTN_KV_HEADS)),
                               cos_t, sin_t)
        attn = _attention(q, k, v, n_ctx)
        r, vv, kk, gate, bonus, lw, kd, bb = _rwkv_features(
            rw, shift_taps[l], decay_base[l], _pad_lora(decay_up[l]), iclr_base[l], _pad_lora(iclr_up[l]),
            gate_up[l].astype(BF16), row(k_k[l]), row(k_a[l]), row(r_k[l]))
        y = _rwkv_scan(r, vv, kk, lw, kd, bb, n_ctx)
        xs = _mixer_out(y, bonus, gate, attn, xs, row(ln_x_w[l]), row(ln_x_b[l]), w_out[l].astype(BF16), mod)
        hq, s1, s2 = _peer_scores(xs, mod, row(norm_ffn[l]), peer_query[l].astype(BF16),
                                  peer_subkeys1[l].astype(BF16), peer_subkeys2[l].astype(BF16))
        thr, lse = _peer_topk(s1, s2)
        xs = _peer_experts(hq, s1, s2, thr, lse, expert_u[l].astype(BF16), expert_v[l].T.astype(BF16), xs, mod)
    return xs[:, n_ctx:, :]
```

```python
import functools
import math

import jax
import jax.numpy as jnp
from jax import lax
from jax.experimental import pallas as pl
from jax.experimental.pallas import tpu as pltpu

F32, BF16 = jnp.float32, jnp.bfloat16
HIGHEST = lax.Precision.HIGHEST

HEAD_DIM = 64
ATTN_HEADS = 8
ATTN_KV_HEADS = 2
ATTN_GROUP = ATTN_HEADS // ATTN_KV_HEADS
ATTN_DIM = ATTN_HEADS * HEAD_DIM
KV_DIM = ATTN_KV_HEADS * HEAD_DIM
ATTN_COLS = ATTN_DIM + 2 * KV_DIM
GRID_W = 64
ROPE_THETA = 10000.0
AXIS_ROT_DIM = HEAD_DIM // 2
AXIS_FREQS = AXIS_ROT_DIM // 2
RWKV_DIM = 512
LORA_COLS = 128
N_DIRS = 2
RWKV_COLS = 3 * RWKV_DIM + 3 * LORA_COLS
N_KEYS = 128
PEER_HEADS = 8
PEER_HALF = 128
PEER_TOPK = 16
N_MOD = 6
NORM_EPS = 1e-6
GN_EPS = 64e-5
L2_EPS = 1e-12

LANES = 128
TOKEN_TILE = 256
KV_TILE = 768
CHUNK = 64
EXPERT_TILE = 1024
EXPERT_SUB = 256
VMEM_LIMIT = 48 * 1024 * 1024


def _params(sem):
    return pltpu.CompilerParams(dimension_semantics=sem, vmem_limit_bytes=VMEM_LIMIT)


def _iota(shape, axis):
    return lax.broadcasted_iota(jnp.int32, shape, axis)


def _head_ones(n):
    return ((_iota((n, n), 0) >> 6) == (_iota((n, n), 1) >> 6)).astype(BF16)


def _head_sum(x, ones):
    n = ones.shape[0]
    outs = []
    for c0 in range(0, x.shape[1], n):
        rest = x[:, c0:c0 + n]
        acc = None
        for _ in range(3):
            part = rest.astype(BF16)
            rest = rest - part.astype(F32)
            t = jnp.dot(part, ones, preferred_element_type=F32)
            acc = t if acc is None else acc + t
        outs.append(acc)
    return outs[0] if len(outs) == 1 else jnp.concatenate(outs, axis=1)


def _norm_mod(x, gain, shift, scale):
    ms = jnp.mean(x * x, axis=-1, keepdims=True)
    return (x * lax.rsqrt(ms + NORM_EPS) * gain) * (1.0 + scale) + shift


def _mod_kernel(c_ref, w_ref, b_ref, o_ref):
    c = c_ref[...]
    a = c * jax.nn.sigmoid(c)
    o_ref[...] = jnp.dot(a, w_ref[...], preferred_element_type=F32, precision=HIGHEST) + b_ref[...]


def _modulation(cc, mod_w, mod_b):
    depth, d, n = mod_w.shape
    return pl.pallas_call(
        _mod_kernel,
        grid=(depth, n // d),
        in_specs=[pl.BlockSpec((8, d), lambda l, j: (0, 0)),
                  pl.BlockSpec((None, d, d), lambda l, j: (l, 0, j)),
                  pl.BlockSpec((None, 1, d), lambda l, j: (l, 0, j))],
        out_specs=pl.BlockSpec((None, 8, d), lambda l, j: (l, 0, j)),
        out_shape=jax.ShapeDtypeStruct((depth, 8, n), F32),
        compiler_params=_params(("parallel", "parallel")),
        name="modulation",
    )(cc, mod_w, mod_b.reshape(depth, 1, n))


def _mod_spec(batch, chunk, d):
    return pl.BlockSpec((None, 1, d), lambda b, i, *_: (jnp.where(i == 0, batch, b), 0, chunk))


def _rope(x, cos, sin):
    w = x.shape[1]
    reps = w // LANES
    if reps > 1:
        cos = jnp.concatenate([cos] * reps, axis=1)
        sin = jnp.concatenate([sin] * reps, axis=1)
    first = (_iota(x.shape, 1) & AXIS_FREQS) == 0
    partner = jnp.where(first, pltpu.roll(x, w - AXIS_FREQS, 1), pltpu.roll(x, AXIS_FREQS, 1))
    return x * cos + partner * sin


def _head_norm(x, gain, ones):
    ms = _head_sum(x * x, ones) * (1.0 / HEAD_DIM)
    return x * lax.rsqrt(ms + NORM_EPS) * gain


def _in_kernel(x_ref, gain_ref, sh_ref, sc_ref, w_ref, qg_ref, kg_ref, cos_ref, sin_ref,
               q_ref, k_ref, v_ref, rw_ref):
    h = _norm_mod(x_ref[...], gain_ref[...], sh_ref[...], sc_ref[...])
    p = jnp.dot(h.astype(BF16), w_ref[...], preferred_element_type=F32)
    cos, sin = cos_ref[...], sin_ref[...]
    q = _head_norm(p[:, :ATTN_DIM], qg_ref[...], _head_ones(256))
    q_ref[...] = (_rope(q, cos, sin) * (HEAD_DIM ** -0.5)).astype(BF16)
    k = _head_norm(p[:, ATTN_DIM:ATTN_DIM + KV_DIM], kg_ref[...], _head_ones(KV_DIM))
    k_ref[...] = _rope(k, cos, sin).astype(BF16)
    v_ref[...] = p[:, ATTN_DIM + KV_DIM:ATTN_COLS].astype(BF16)
    rw_ref[...] = p[:, ATTN_COLS:]


def _in_proj(xs, mod, gain, w_in, q_gain, k_gain, cos_t, sin_t):
    b, t, d = xs.shape
    tm = TOKEN_TILE
    n = w_in.shape[1]
    tile = lambda w: pl.BlockSpec((None, tm, w), lambda bb, i: (bb, i, 0))
    row = lambda w: pl.BlockSpec((1, w), lambda bb, i: (0, 0))
    return pl.pallas_call(
        _in_kernel,
        grid=(b, t // tm),
        in_specs=[tile(d), row(d), _mod_spec(b, 0, d), _mod_spec(b, 1, d),
                  pl.BlockSpec((d, n), lambda bb, i: (0, 0)), row(ATTN_DIM), row(KV_DIM),
                  pl.BlockSpec((tm, LANES), lambda bb, i: (i, 0)),
                  pl.BlockSpec((tm, LANES), lambda bb, i: (i, 0))],
        out_specs=[tile(ATTN_DIM), tile(KV_DIM), tile(KV_DIM), tile(RWKV_COLS)],
        out_shape=[jax.ShapeDtypeStruct((b, t, ATTN_DIM), BF16),
                   jax.ShapeDtypeStruct((b, t, KV_DIM), BF16),
                   jax.ShapeDtypeStruct((b, t, KV_DIM), BF16),
                   jax.ShapeDtypeStruct((b, t, RWKV_COLS), F32)],
        compiler_params=_params(("parallel", "parallel")),
        name="in_proj",
    )(xs, gain, mod, mod, w_in, q_gain, k_gain, cos_t, sin_t)


def _attn_step(qs_ref, k_ref, v_ref, m_ref, l_ref, acc_ref, n_valid):
    for g in range(ATTN_KV_HEADS):
        k = k_ref[:, g * HEAD_DIM:(g + 1) * HEAD_DIM]
        v = v_ref[:, g * HEAD_DIM:(g + 1) * HEAD_DIM]
        s = lax.dot_general(qs_ref[g], k, (((1,), (1,)), ((), ())), preferred_element_type=F32)
        if n_valid is not None:
            s = jnp.where(_iota(s.shape, 1) < n_valid, s, -jnp.inf)
        m_old = m_ref[g]
        m_new = jnp.maximum(m_old, jnp.max(s, axis=1, keepdims=True))
        alpha = jnp.exp(m_old - m_new)
        p = jnp.exp(s - m_new)
        l_ref[g] = alpha * l_ref[g] + jnp.sum(p, axis=1, keepdims=True)
        acc_ref[g] = alpha * acc_ref[g] + jnp.dot(p.astype(BF16), v, preferred_element_type=F32)
        m_ref[g] = m_new


def _attn_kernel(q_ref, k_ref, v_ref, o_ref, qs_ref, m_ref, l_ref, acc_ref, *, n_ctx, nk):
    qi, kj = pl.program_id(1), pl.program_id(2)
    tq = q_ref.shape[0]

    @pl.when(kj == 0)
    def _():
        for hh in range(ATTN_HEADS):
            g, h = divmod(hh, ATTN_GROUP)
            qs_ref[g, h * tq:(h + 1) * tq, :] = q_ref[:, hh * HEAD_DIM:(hh + 1) * HEAD_DIM]
        m_ref[...] = jnp.full(m_ref.shape, -jnp.inf, F32)
        l_ref[...] = jnp.zeros(l_ref.shape, F32)
        acc_ref[...] = jnp.zeros(acc_ref.shape, F32)

    @pl.when((qi == 0) & (kj == 0))
    def _():
        _attn_step(qs_ref, k_ref, v_ref, m_ref, l_ref, acc_ref, n_ctx)

    @pl.when(qi > 0)
    def _():
        _attn_step(qs_ref, k_ref, v_ref, m_ref, l_ref, acc_ref, None)

    @pl.when(kj == nk - 1)
    def _():
        for hh in range(ATTN_HEADS):
            g, h = divmod(hh, ATTN_GROUP)
            o = acc_ref[g, h * tq:(h + 1) * tq, :] / l_ref[g, h * tq:(h + 1) * tq, :]
            o_ref[:, hh * HEAD_DIM:(hh + 1) * HEAD_DIM] = o.astype(BF16)


def _attention(q, k, v, n_ctx):
    b, t, _ = q.shape
    tq = TOKEN_TILE
    tk = KV_TILE if t % KV_TILE == 0 else TOKEN_TILE
    nk = t // tk
    assert n_ctx == tq and n_ctx <= tk
    kv_spec = pl.BlockSpec((None, tk, KV_DIM), lambda bb, i, j: (bb, jnp.where(i == 0, 0, j), 0))
    return pl.pallas_call(
        functools.partial(_attn_kernel, n_ctx=n_ctx, nk=nk),
        grid=(b, t // tq, nk),
        in_specs=[pl.BlockSpec((None, tq, ATTN_DIM), lambda bb, i, j: (bb, i, 0)), kv_spec, kv_spec],
        out_specs=pl.BlockSpec((None, tq, ATTN_DIM), lambda bb, i, j: (bb, i, 0)),
        out_shape=jax.ShapeDtypeStruct((b, t, ATTN_DIM), BF16),
        scratch_shapes=[pltpu.VMEM((ATTN_KV_HEADS, ATTN_GROUP * tq, HEAD_DIM), BF16),
                        pltpu.VMEM((ATTN_KV_HEADS, ATTN_GROUP * tq, 1), F32),
                        pltpu.VMEM((ATTN_KV_HEADS, ATTN_GROUP * tq, 1), F32),
                        pltpu.VMEM((ATTN_KV_HEADS, ATTN_GROUP * tq, HEAD_DIM), F32)],
        compiler_params=_params(("parallel", "parallel", "arbitrary")),
        name="attention",
    )(q, k, v)


def _feat_kernel(rw_ref, prev_ref, next_ref, taps_ref, dbase_ref, dup_ref, ibase_ref, iup_ref, gup_ref,
                 kk_par_ref, ka_ref, rk_ref,
                 r_out, v_out, kk_out, gate_out, bonus_out, lw_out, kd_out, bb_out, *, nt):
    i = pl.program_id(1)
    x = rw_ref[...]
    tm = x.shape[0]
    prev_row = jnp.where(i <= 1, 0.0, prev_ref[7:8, :])
    next_row = jnp.where((i == 0) | (i == nt - 1), 0.0, next_ref[0:1, :])
    row = _iota(x.shape, 0)
    xm = jnp.where(row == 0, prev_row, pltpu.roll(x, 1, 0))
    xp = jnp.where(row == tm - 1, next_row, pltpu.roll(x, tm - 1, 0))
    xs = xm * taps_ref[0:1, :] + x * taps_ref[1:2, :] + xp * taps_ref[2:3, :]

    o3 = 3 * RWKV_DIM
    r, k, v = xs[:, :RWKV_DIM], xs[:, RWKV_DIM:2 * RWKV_DIM], xs[:, 2 * RWKV_DIM:o3]
    wd = jnp.tanh(xs[:, o3:o3 + LORA_COLS]).astype(BF16)
    ad = xs[:, o3 + LORA_COLS:o3 + 2 * LORA_COLS].astype(BF16)
    gd = jax.nn.sigmoid(xs[:, o3 + 2 * LORA_COLS:]).astype(BF16)
    ones = _head_ones(256)
    kkr = k * kk_par_ref[...]
    kk = kkr * lax.rsqrt(_head_sum(kkr * kkr, ones) + L2_EPS)
    r_out[...] = r
    v_out[...] = v
    kk_out[...] = kk
    gate_out[...] = jnp.dot(gd, gup_ref[...], preferred_element_type=F32)
    bonus_out[...] = _head_sum(r * k * rk_ref[...], ones) * v
    for d in range(N_DIRS):
        z = dbase_ref[d:d + 1, :] + jnp.dot(wd, dup_ref[d], preferred_element_type=F32)
        lw_out[d] = jax.nn.sigmoid(z) * (-math.exp(-0.5))
        iclr = jax.nn.sigmoid(ibase_ref[d:d + 1, :] + jnp.dot(ad, iup_ref[d], preferred_element_type=F32))
        kd_out[d] = k * (1.0 + (iclr - 1.0) * ka_ref[...])
        bb_out[d] = kk * iclr


def _rwkv_features(rw, taps, dbase, dup, ibase, iup, gup, k_k, k_a, r_k):
    b, t, w = rw.shape
    tm = TOKEN_TILE
    nt = t // tm
    r8 = tm // 8
    full = lambda a: pl.BlockSpec(a.shape, lambda bb, i: (0,) * a.ndim)
    tile = pl.BlockSpec((None, tm, RWKV_DIM), lambda bb, i: (bb, i, 0))
    dtile = pl.BlockSpec((None, N_DIRS, tm, RWKV_DIM), lambda bb, i: (bb, 0, i, 0))
    shared = jax.ShapeDtypeStruct((b, t, RWKV_DIM), F32)
    perdir = jax.ShapeDtypeStruct((b, N_DIRS, t, RWKV_DIM), F32)
    consts = (taps, dbase, dup, ibase, iup, gup, k_k, k_a, r_k)
    return pl.pallas_call(
        functools.partial(_feat_kernel, nt=nt),
        grid=(b, nt),
        in_specs=[pl.BlockSpec((None, tm, w), lambda bb, i: (bb, i, 0)),
                  pl.BlockSpec((None, 8, w), lambda bb, i: (bb, jnp.maximum(i * r8 - 1, 0), 0)),
                  pl.BlockSpec((None, 8, w), lambda bb, i: (bb, jnp.minimum((i + 1) * r8, nt * r8 - 1), 0))]
                 + [full(a) for a in consts],
        out_specs=[tile] * 5 + [dtile] * 3,
        out_shape=[shared] * 5 + [perdir] * 3,
        compiler_params=_params(("parallel", "parallel")),
        name="rwkv_features",
    )(rw, rw, rw, *consts)


def _scan_kernel(r_ref, v_ref, kk_ref, lw_ref, kd_ref, bb_ref, y_ref, st_ref):
    rev = pl.program_id(1) == 1
    c = pl.program_id(2)

    @pl.when(c == 0)
    def _():
        st_ref[...] = jnp.zeros(st_ref.shape, F32)

    ch, pw = CHUNK, LANES
    ti, si = _iota((ch, ch), 0), _iota((ch, ch), 1)
    tri = (jnp.where(rev, si - ti, ti - si) >= 0).astype(F32)
    a, bcol = _iota((pw, pw), 0), _iota((pw, pw), 1)
    same = (a >> 6) == (bcol >> 6)
    ahead = jnp.where(rev, a - bcol, bcol - a)
    strict = same & (ahead > 0)
    incl = same & (ahead >= 0)
    eye = a == bcol
    head0 = _iota((ch, pw), 1) < HEAD_DIM

    def stack(x):
        return jnp.concatenate([jnp.where(head0, x, 0.0), jnp.where(head0, 0.0, x)], axis=0)

    def dup(x):
        return jnp.concatenate([x, x], axis=0)

    def mm(x, y):
        return jnp.dot(x.astype(BF16), y.astype(BF16), preferred_element_type=F32)

    pairs = range(RWKV_DIM // pw)
    sls = [slice(p * pw, (p + 1) * pw) for p in pairs]
    each = lambda f, *cols: [f(*args) for args in zip(*cols)]
    lw = [lw_ref[:, sl] for sl in sls]
    cum = each(lambda a_: jnp.dot(tri, a_, preferred_element_type=F32, precision=HIGHEST), lw)
    tot = each(lambda a_: jnp.sum(a_, axis=0, keepdims=True), lw)
    w_inv = each(lambda c_: jnp.exp(-c_), cum)
    w_rem = each(lambda t_, c_: jnp.exp(t_ - c_), tot, cum)
    at = each(lambda sl, c_, l_: -kk_ref[:, sl] * jnp.exp(c_ - l_), sls, cum, lw)
    rt = each(lambda sl, c_: r_ref[:, sl] * jnp.exp(c_), sls, cum)
    bt = each(lambda sl, w_: bb_ref[:, sl] * w_, sls, w_inv)
    kt = each(lambda sl, w_: kd_ref[:, sl] * w_, sls, w_inv)
    bw = each(lambda sl, w_: bb_ref[:, sl] * w_, sls, w_rem)
    kw = each(lambda sl, w_: kd_ref[:, sl] * w_, sls, w_rem)

    nt_dims = (((1,), (1,)), ((), ()))
    m = each(lambda b_, k_, a_, r_: lax.dot_general(
        jnp.concatenate([stack(b_), stack(k_)], axis=0).astype(BF16),
        jnp.concatenate([dup(a_), dup(r_)], axis=0).astype(BF16), nt_dims, preferred_element_type=F32),
        bt, kt, at, rt)
    mab = each(lambda m_: jnp.where(strict, m_[:pw, :pw], 0.0), m)
    mbr = each(lambda m_: jnp.where(incl, m_[:pw, pw:], 0.0), m)
    mak = each(lambda m_: jnp.where(strict, m_[pw:, :pw], 0.0), m)
    mkr = each(lambda m_: jnp.where(incl, m_[pw:, pw:], 0.0), m)

    n = mab
    x = each(lambda b_, m_: jnp.concatenate([stack(b_), m_], axis=1), bw, mbr)
    levels = CHUNK.bit_length() - 1
    for lvl in range(levels - 1):
        nx = each(lambda n_, x_: mm(n_, jnp.concatenate([n_, x_], axis=1)), n, x)
        n = [nx_[:, :pw] for nx_ in nx]
        x = each(lambda x_, nx_: x_ + nx_[:, pw:], x, nx)
    x = each(lambda n_, x_: x_ + mm(n_, x_), n, x)
    z = [x_[:, :pw] for x_ in x]
    tm_ = [x_[:, pw:] for x_ in x]

    a2 = each(stack, at)
    v2 = each(lambda sl: stack(v_ref[:, sl]), sls)
    pt = each(lambda z_, a_, t_: mm(z_.T, a_) + jnp.where(eye, jnp.exp(t_), 0.0), z, a2, tot)
    dd = each(lambda k_, z_, w_: mm(k_, z_) + stack(w_), mak, z, kw)
    qt = each(lambda d_, v_: mm(d_.T, v_), dd, v2)
    g = each(lambda r_, t_, a_: stack(r_) + mm(t_.T, a_), rt, tm_, a2)
    hm = each(lambda kr_, k_, t_: kr_ + mm(k_, t_), mkr, mak, tm_)
    hv = each(lambda h_, v_: mm(h_.T, v_), hm, v2)
    st = [st_ref[p] for p in pairs]
    ys = each(lambda g_, s_, h_: mm(g_, s_) + h_, g, st, hv)
    new = each(lambda p_, s_, q_: mm(p_, s_) + q_, pt, st, qt)
    for p in pairs:
        y_ref[:, sls[p]] = ys[p][:ch] + ys[p][ch:]
        st_ref[p] = new[p]


def _rwkv_scan(r, v, kk, lw, kd, bb, n_ctx):
    b, t, w = r.shape
    nc, ncx = t // CHUNK, n_ctx // CHUNK

    def chunk(d, c):
        return jnp.where(d == 0, c, jnp.where(c < ncx, ncx - 1 - c, nc - 1 + ncx - c))

    shared = pl.BlockSpec((None, CHUNK, w), lambda bb_, d, c: (bb_, chunk(d, c), 0))
    perdir = pl.BlockSpec((None, None, CHUNK, w), lambda bb_, d, c: (bb_, d, chunk(d, c), 0))
    return pl.pallas_call(
        _scan_kernel,
        grid=(b, N_DIRS, nc),
        in_specs=[shared] * 3 + [perdir] * 3,
        out_specs=perdir,
        out_shape=jax.ShapeDtypeStruct((b, N_DIRS, t, w), F32),
        scratch_shapes=[pltpu.VMEM((w // LANES, LANES, LANES), F32)],
        compiler_params=_params(("parallel", "parallel", "arbitrary")),
        name="rwkv_scan",
    )(r, v, kk, lw, kd, bb)


def _out_kernel(y0_ref, y1_ref, bonus_ref, gate_ref, attn_ref, x_ref, lnw_ref, lnb_ref, w_ref, g_ref, o_ref):
    ones = _head_ones(256)
    y = y0_ref[...] + y1_ref[...]
    yc = y - _head_sum(y, ones) * (1.0 / HEAD_DIM)
    var = _head_sum(yc * yc, ones) * (1.0 / HEAD_DIM)
    yn = yc * lax.rsqrt(var + GN_EPS) * lnw_ref[...] + lnb_ref[...]
    rw = ((yn + bonus_ref[...]) * gate_ref[...]).astype(BF16)
    mix = jnp.dot(jnp.concatenate([attn_ref[...], rw], axis=1), w_ref[...], preferred_element_type=F32)
    o_ref[...] = x_ref[...] + g_ref[...] * mix


def _mixer_out(y, bonus, gate, attn, xs, ln_w, ln_b, w_out, mod):
    b, t, d = xs.shape
    tm = TOKEN_TILE
    tile = lambda w: pl.BlockSpec((None, tm, w), lambda bb, i: (bb, i, 0))
    ydir = lambda dd: pl.BlockSpec((None, None, tm, RWKV_DIM), lambda bb, i: (bb, dd, i, 0))
    row = lambda w: pl.BlockSpec((1, w), lambda bb, i: (0, 0))
    return pl.pallas_call(
        _out_kernel,
        grid=(b, t // tm),
        in_specs=[ydir(0), ydir(1), tile(RWKV_DIM), tile(RWKV_DIM), tile(ATTN_DIM), tile(d),
                  row(RWKV_DIM), row(RWKV_DIM), pl.BlockSpec(w_out.shape, lambda bb, i: (0, 0)),
                  _mod_spec(b, 2, d)],
        out_specs=tile(d),
        out_shape=jax.ShapeDtypeStruct((b, t, d), F32),
        compiler_params=_params(("parallel", "parallel")),
        name="mixer_out",
    )(y, y, bonus, gate, attn, xs, ln_w, ln_b, w_out, mod)


def _query_kernel(x_ref, gain_ref, sh_ref, sc_ref, w_ref, k1_ref, k2_ref, ht_ref, s1_ref, s2_ref):
    hf = _norm_mod(x_ref[...], gain_ref[...], sh_ref[...], sc_ref[...])
    ht_ref[...] = hf.T.astype(BF16)
    q = jnp.dot(hf.astype(BF16), w_ref[...], preferred_element_type=F32).astype(BF16)
    nt = (((1,), (1,)), ((), ()))
    for hd in range(PEER_HEADS):
        c0 = hd * 2 * PEER_HALF
        s1_ref[hd] = lax.dot_general(k1_ref[...], q[:, c0:c0 + PEER_HALF], nt, preferred_element_type=F32)
        s2_ref[hd] = lax.dot_general(k2_ref[...], q[:, c0 + PEER_HALF:c0 + 2 * PEER_HALF], nt,
                                     preferred_element_type=F32)


def _peer_scores(xs, mod, gain, w_query, keys1, keys2):
    b, t, d = xs.shape
    tm = TOKEN_TILE
    nt = t // tm
    row = lambda w: pl.BlockSpec((1, w), lambda bb, i: (0, 0))
    full = lambda a: pl.BlockSpec(a.shape, lambda bb, i: (0,) * a.ndim)
    sspec = pl.BlockSpec((PEER_HEADS, N_KEYS, tm), lambda bb, i: (0, 0, bb * nt + i))
    sshape = jax.ShapeDtypeStruct((PEER_HEADS, N_KEYS, b * t), F32)
    return pl.pallas_call(
        _query_kernel,
        grid=(b, nt),
        in_specs=[pl.BlockSpec((None, tm, d), lambda bb, i: (bb, i, 0)), row(d), _mod_spec(b, 3, d),
                  _mod_spec(b, 4, d), full(w_query), full(keys1), full(keys2)],
        out_specs=[pl.BlockSpec((d, tm), lambda bb, i: (0, bb * nt + i)), sspec, sspec],
        out_shape=[jax.ShapeDtypeStruct((d, b * t), BF16), sshape, sshape],
        compiler_params=_params(("parallel", "parallel")),
        name="peer_scores",
    )(xs, gain, mod, mod, w_query, keys1, keys2)


def _top_rows(x_ref, out_ref):
    slabs, n, tl = x_ref.shape
    idx = _iota((n, tl), 0).astype(F32)

    def body(i, carry):
        xs = [x_ref[s] for s in range(slabs)]
        ms = [jnp.max(x, axis=0, keepdims=True) for x in xs]
        firsts = [jnp.min(jnp.where(x == m, idx, float(n)), axis=0, keepdims=True) for x, m in zip(xs, ms)]
        for s in range(slabs):
            out_ref[s, pl.ds(i, 1), :] = ms[s]
            x_ref[s] = jnp.where(idx == firsts[s], -jnp.inf, xs[s])
        return carry

    lax.fori_loop(0, PEER_TOPK, body, 0)


_CAND_COUNTS = [PEER_TOPK // (i + 1) for i in range(PEER_TOPK)]
_CAND_ROWS = -(-sum(_CAND_COUNTS) // 8) * 8


def _topk_kernel(s1_ref, s2_ref, thr_ref, lse_ref, work_ref, v_ref, cand_ref, top_ref):
    work_ref[:PEER_HEADS] = s1_ref[...]
    work_ref[PEER_HEADS:] = s2_ref[...]
    _top_rows(work_ref, v_ref)
    cand_ref[...] = jnp.full(cand_ref.shape, -jnp.inf, F32)
    for hd in range(PEER_HEADS):
        off = 0
        for i, cnt in enumerate(_CAND_COUNTS):
            cand_ref[hd, off:off + cnt, :] = v_ref[hd, i:i + 1, :] + v_ref[PEER_HEADS + hd, 0:cnt, :]
            off += cnt
    _top_rows(cand_ref, top_ref)
    for hd in range(PEER_HEADS):
        top = top_ref[hd]
        mx = top[0:1, :]
        thr_ref[hd:hd + 1, :] = top[PEER_TOPK - 1:PEER_TOPK, :]
        lse_ref[hd:hd + 1, :] = mx + jnp.log(jnp.sum(jnp.exp(top - mx), axis=0, keepdims=True))


def _peer_topk(s1, s2):
    _, _, n = s1.shape
    tl = TOKEN_TILE
    sspec = pl.BlockSpec((PEER_HEADS, N_KEYS, tl), lambda i: (0, 0, i))
    ospec = pl.BlockSpec((PEER_HEADS, tl), lambda i: (0, i))
    oshape = jax.ShapeDtypeStruct((PEER_HEADS, n), F32)
    return pl.pallas_call(
        _topk_kernel,
        grid=(n // tl,),
        in_specs=[sspec, sspec],
        out_specs=[ospec, ospec],
        out_shape=[oshape, oshape],
        scratch_shapes=[pltpu.VMEM((2 * PEER_HEADS, N_KEYS, tl), F32),
                        pltpu.VMEM((2 * PEER_HEADS, PEER_TOPK, tl), F32),
                        pltpu.VMEM((PEER_HEADS, _CAND_ROWS, tl), F32),
                        pltpu.VMEM((PEER_HEADS, PEER_TOPK, tl), F32)],
        compiler_params=_params(("parallel",)),
        name="peer_topk",
    )(s1, s2)


def _peer_kernel(ht_ref, s1_ref, s2_ref, thr_ref, lse_ref, u_ref, vt_ref, x_ref, g_ref, o_ref, acc_ref, *, n_steps):
    ap = pl.program_id(2)

    @pl.when(ap == 0)
    def _():
        acc_ref[...] = jnp.zeros(acc_ref.shape, F32)

    thr = [thr_ref[hd:hd + 1, :] for hd in range(PEER_HEADS)]
    lse = [lse_ref[hd:hd + 1, :] for hd in range(PEER_HEADS)]
    ht = ht_ref[...]
    sub, rows = EXPERT_SUB, EXPERT_SUB // N_KEYS

    def pre_act(j):
        return jnp.dot(u_ref[j * sub:(j + 1) * sub, :], ht, preferred_element_type=F32)

    pre, upd = pre_act(0), None
    for j in range(EXPERT_TILE // sub):
        nxt = pre_act(j + 1) if (j + 1) * sub < EXPERT_TILE else None
        gates = []
        for r in range(rows):
            a = ap * (EXPERT_TILE // N_KEYS) + j * rows + r
            w = None
            for hd in range(PEER_HEADS):
                pair = s1_ref[hd, pl.ds(a, 1), :] + s2_ref[hd]
                e = jnp.where(pair >= thr[hd], jnp.exp(pair - lse[hd]), 0.0)
                w = e if w is None else w + e
            gates.append(w)
        act = 0.5 * pre * (1.0 + lax.erf(pre * (2.0 ** -0.5)))
        gw = (jnp.concatenate(gates, axis=0) * act).astype(BF16)
        part = jnp.dot(vt_ref[:, j * sub:(j + 1) * sub], gw, preferred_element_type=F32)
        upd = part if upd is None else upd + part
        pre = nxt
    acc_ref[...] += upd

    @pl.when(ap == n_steps - 1)
    def _():
        o_ref[...] = x_ref[...] + g_ref[...] * acc_ref[...].T


def _peer_experts(ht, s1, s2, thr, lse, u, vt, xs, mod):
    b, t, d = xs.shape
    tm = TOKEN_TILE
    nt = t // tm
    n_steps = u.shape[0] // EXPERT_TILE
    tok = lambda bb, i, a: bb * nt + i
    sspec = pl.BlockSpec((PEER_HEADS, N_KEYS, tm), lambda bb, i, a: (0, 0, tok(bb, i, a)))
    rspec = pl.BlockSpec((PEER_HEADS, tm), lambda bb, i, a: (0, tok(bb, i, a)))
    tile = pl.BlockSpec((None, tm, d), lambda bb, i, a: (bb, i, 0))
    return pl.pallas_call(
        functools.partial(_peer_kernel, n_steps=n_steps),
        grid=(b, nt, n_steps),
        in_specs=[pl.BlockSpec((d, tm), lambda bb, i, a: (0, tok(bb, i, a))), sspec, sspec, rspec, rspec,
                  pl.BlockSpec((EXPERT_TILE, d), lambda bb, i, a: (a, 0)),
                  pl.BlockSpec((d, EXPERT_TILE), lambda bb, i, a: (0, a)),
                  tile, _mod_spec(b, 5, d)],
        out_specs=tile,
        out_shape=jax.ShapeDtypeStruct((b, t, d), F32),
        scratch_shapes=[pltpu.VMEM((d, tm), F32)],
        compiler_params=_params(("parallel", "parallel", "arbitrary")),
        name="peer_experts",
    )(ht, s1, s2, thr, lse, u, vt, xs, mod)


def _rope_tables(n_lat, n_ctx):
    pos = jnp.arange(n_lat)
    inv_freq = ROPE_THETA ** (-jnp.arange(AXIS_FREQS, dtype=F32) * 2.0 / AXIS_ROT_DIM)
    ang = jnp.stack([(pos // GRID_W).astype(F32)[:, None] * inv_freq,
                     (pos % GRID_W).astype(F32)[:, None] * inv_freq], axis=1)
    cos = jnp.broadcast_to(jnp.cos(ang)[:, :, None, :], (n_lat, 2, 2, AXIS_FREQS)).reshape(n_lat, HEAD_DIM)
    sin = jnp.stack([-jnp.sin(ang), jnp.sin(ang)], axis=2).reshape(n_lat, HEAD_DIM)
    cos = jnp.concatenate([jnp.ones((n_ctx, HEAD_DIM), F32), cos], axis=0)
    sin = jnp.concatenate([jnp.zeros((n_ctx, HEAD_DIM), F32), sin], axis=0)
    return jnp.tile(cos, (1, LANES // HEAD_DIM)), jnp.tile(sin, (1, LANES // HEAD_DIM))


def _pad_lora(up):
    z = jnp.zeros_like(up[0])
    return jnp.stack([jnp.concatenate([up[0], z], axis=0), jnp.concatenate([z, up[1]], axis=0)]).astype(BF16)


def kernel(x, c, ctx, c_ctx, mod_w, mod_b, norm_mix, norm_ffn, w_in, q_gain, k_gain, shift_taps, decay_base,
           decay_up, iclr_base, iclr_up, gate_up, k_k, k_a, r_k, ln_x_w, ln_x_b, w_out, peer_query, peer_subkeys1,
           peer_subkeys2, expert_u, expert_v):
    b, n_lat, d = x.shape
    n_ctx = ctx.shape[1]
    depth = mod_w.shape[0]
    assert n_ctx == TOKEN_TILE and n_lat % TOKEN_TILE == 0 and b < 8

    xs = jnp.concatenate([ctx, x], axis=1)
    cc = jnp.concatenate([c, c_ctx[None, :], jnp.zeros((8 - b - 1, d), F32)], axis=0)
    mod_all = _modulation(cc, mod_w, mod_b)
    cos_t, sin_t = _rope_tables(n_lat, n_ctx)
    row = lambda a: a.reshape(1, -1)

    for l in range(depth):
        mod = mod_all[l].reshape(8, 1, N_MOD * d)
        q, k, v, rw = _in_proj(xs, mod, row(norm_mix[l]), w_in[l].astype(BF16),
                               row(jnp.tile(q_gain[l], ATTN_HEADS)), row(jnp.tile(k_gain[l], ATTN_KV_HEADS)),
                               cos_t, sin_t)
        attn = _attention(q, k, v, n_ctx)
        r, vv, kk, gate, bonus, lw, kd, bb = _rwkv_features(
            rw, shift_taps[l], decay_base[l], _pad_lora(decay_up[l]), iclr_base[l], _pad_lora(iclr_up[l]),
            gate_up[l].astype(BF16), row(k_k[l]), row(k_a[l]), row(r_k[l]))
        y = _rwkv_scan(r, vv, kk, lw, kd, bb, n_ctx)
        xs = _mixer_out(y, bonus, gate, attn, xs, row(ln_x_w[l]), row(ln_x_b[l]), w_out[l].astype(BF16), mod)
        ht, s1, s2 = _peer_scores(xs, mod, row(norm_ffn[l]), peer_query[l].astype(BF16),
                                  peer_subkeys1[l].astype(BF16), peer_subkeys2[l].astype(BF16))
        thr, lse = _peer_topk(s1, s2)
        xs = _peer_experts(ht, s1, s2, thr, lse, expert_u[l].astype(BF16), expert_v[l].T.astype(BF16), xs, mod)
    return xs[:, n_ctx:, :]
```

```python
import functools
import math

import jax
import jax.numpy as jnp
from jax import lax
from jax.experimental import pallas as pl
from jax.experimental.pallas import tpu as pltpu

F32, BF16 = jnp.float32, jnp.bfloat16
HIGHEST = lax.Precision.HIGHEST

HEAD_DIM = 64
ATTN_HEADS = 8
ATTN_KV_HEADS = 2
ATTN_GROUP = ATTN_HEADS // ATTN_KV_HEADS
ATTN_DIM = ATTN_HEADS * HEAD_DIM
KV_DIM = ATTN_KV_HEADS * HEAD_DIM
ATTN_COLS = ATTN_DIM + 2 * KV_DIM
GRID_W = 64
ROPE_THETA = 10000.0
AXIS_ROT_DIM = HEAD_DIM // 2
AXIS_FREQS = AXIS_ROT_DIM // 2
RWKV_DIM = 512
LORA_COLS = 128
N_DIRS = 2
RWKV_COLS = 3 * RWKV_DIM + 3 * LORA_COLS
N_KEYS = 128
PEER_HEADS = 8
PEER_HALF = 128
PEER_TOPK = 16
N_MOD = 6
NORM_EPS = 1e-6
GN_EPS = 64e-5
L2_EPS = 1e-12

LANES = 128
TOKEN_TILE = 256
KV_TILE = 768
CHUNK = 64
EXPERT_TILE = 1024
EXPERT_SUB = 256
VMEM_LIMIT = 48 * 1024 * 1024


def _params(sem):
    return pltpu.CompilerParams(dimension_semantics=sem, vmem_limit_bytes=VMEM_LIMIT)


def _iota(shape, axis):
    return lax.broadcasted_iota(jnp.int32, shape, axis)


def _head_ones(n):
    return ((_iota((n, n), 0) >> 6) == (_iota((n, n), 1) >> 6)).astype(BF16)


def _head_sum(x, ones):
    n = ones.shape[0]
    outs = []
    for c0 in range(0, x.shape[1], n):
        rest = x[:, c0:c0 + n]
        acc = None
        for _ in range(3):
            part = rest.astype(BF16)
            rest = rest - part.astype(F32)
            t = jnp.dot(part, ones, preferred_element_type=F32)
            acc = t if acc is None else acc + t
        outs.append(acc)
    return outs[0] if len(outs) == 1 else jnp.concatenate(outs, axis=1)


def _norm_mod(x, gain, shift, scale):
    ms = jnp.mean(x * x, axis=-1, keepdims=True)
    return (x * lax.rsqrt(ms + NORM_EPS) * gain) * (1.0 + scale) + shift


def _mod_kernel(c_ref, w_ref, b_ref, o_ref):
    c = c_ref[...]
    a = c * jax.nn.sigmoid(c)
    o_ref[...] = jnp.dot(a, w_ref[...], preferred_element_type=F32, precision=HIGHEST) + b_ref[...]


def _modulation(cc, mod_w, mod_b):
    depth, d, n = mod_w.shape
    return pl.pallas_call(
        _mod_kernel,
        grid=(depth, n // d),
        in_specs=[pl.BlockSpec((8, d), lambda l, j: (0, 0)),
                  pl.BlockSpec((None, d, d), lambda l, j: (l, 0, j)),
                  pl.BlockSpec((None, 1, d), lambda l, j: (l, 0, j))],
        out_specs=pl.BlockSpec((None, 8, d), lambda l, j: (l, 0, j)),
        out_shape=jax.ShapeDtypeStruct((depth, 8, n), F32),
        compiler_params=_params(("parallel", "parallel")),
        name="modulation",
    )(cc, mod_w, mod_b.reshape(depth, 1, n))


def _mod_spec(batch, chunk, d):
    return pl.BlockSpec((None, 1, d), lambda b, i, *_: (jnp.where(i == 0, batch, b), 0, chunk))


def _rope(x, cos, sin):
    w = x.shape[1]
    reps = w // LANES
    if reps > 1:
        cos = jnp.concatenate([cos] * reps, axis=1)
        sin = jnp.concatenate([sin] * reps, axis=1)
    first = (_iota(x.shape, 1) & AXIS_FREQS) == 0
    partner = jnp.where(first, pltpu.roll(x, w - AXIS_FREQS, 1), pltpu.roll(x, AXIS_FREQS, 1))
    return x * cos + partner * sin


def _head_norm(x, gain, ones):
    ms = _head_sum(x * x, ones) * (1.0 / HEAD_DIM)
    return x * lax.rsqrt(ms + NORM_EPS) * gain


def _in_kernel(x_ref, gain_ref, sh_ref, sc_ref, w_ref, qg_ref, kg_ref, cos_ref, sin_ref,
               q_ref, k_ref, vt_ref, rw_ref):
    h = _norm_mod(x_ref[...], gain_ref[...], sh_ref[...], sc_ref[...])
    p = jnp.dot(h.astype(BF16), w_ref[...], preferred_element_type=F32)
    cos, sin = cos_ref[...], sin_ref[...]
    q = _head_norm(p[:, :ATTN_DIM], qg_ref[...], _head_ones(256))
    q_ref[...] = (_rope(q, cos, sin) * (HEAD_DIM ** -0.5)).astype(BF16)
    k = _head_norm(p[:, ATTN_DIM:ATTN_DIM + KV_DIM], kg_ref[...], _head_ones(KV_DIM))
    k_ref[...] = _rope(k, cos, sin).astype(BF16)
    vt_ref[...] = p[:, ATTN_DIM + KV_DIM:ATTN_COLS].T.astype(BF16)
    rw_ref[...] = p[:, ATTN_COLS:]


def _in_proj(xs, mod, gain, w_in, q_gain, k_gain, cos_t, sin_t):
    b, t, d = xs.shape
    tm = TOKEN_TILE
    n = w_in.shape[1]
    tile = lambda w: pl.BlockSpec((None, tm, w), lambda bb, i: (bb, i, 0))
    row = lambda w: pl.BlockSpec((1, w), lambda bb, i: (0, 0))
    return pl.pallas_call(
        _in_kernel,
        grid=(b, t // tm),
        in_specs=[tile(d), row(d), _mod_spec(b, 0, d), _mod_spec(b, 1, d),
                  pl.BlockSpec((d, n), lambda bb, i: (0, 0)), row(ATTN_DIM), row(KV_DIM),
                  pl.BlockSpec((tm, LANES), lambda bb, i: (i, 0)),
                  pl.BlockSpec((tm, LANES), lambda bb, i: (i, 0))],
        out_specs=[tile(ATTN_DIM), tile(KV_DIM), pl.BlockSpec((None, KV_DIM, tm), lambda bb, i: (bb, 0, i)),
                   tile(RWKV_COLS)],
        out_shape=[jax.ShapeDtypeStruct((b, t, ATTN_DIM), BF16),
                   jax.ShapeDtypeStruct((b, t, KV_DIM), BF16),
                   jax.ShapeDtypeStruct((b, KV_DIM, t), BF16),
                   jax.ShapeDtypeStruct((b, t, RWKV_COLS), F32)],
        compiler_params=_params(("parallel", "parallel")),
        name="in_proj",
    )(xs, gain, mod, mod, w_in, q_gain, k_gain, cos_t, sin_t)


def _attn_step(qt_ref, k_ref, vt_ref, m_ref, l_ref, acc_ref, n_valid):
    k = k_ref[...]
    groups = range(ATTN_KV_HEADS)
    s = [jnp.dot(k, qt_ref[g], preferred_element_type=F32) for g in groups]
    if n_valid is not None:
        s = [jnp.where(_iota(sg.shape, 0) < n_valid, sg, -jnp.inf) for sg in s]
    for g in groups:
        m_old = m_ref[g]
        m_new = jnp.maximum(m_old, jnp.max(s[g], axis=0, keepdims=True))
        alpha = jnp.exp(m_old - m_new)
        p = jnp.exp(s[g] - m_new)
        l_ref[g] = alpha * l_ref[g] + jnp.sum(p, axis=0, keepdims=True)
        pv = jnp.dot(vt_ref[g * HEAD_DIM:(g + 1) * HEAD_DIM, :], p.astype(BF16), preferred_element_type=F32)
        acc_ref[g] = alpha * acc_ref[g] + pv
        m_ref[g] = m_new


def _attn_kernel(q_ref, k_ref, vt_ref, o_ref, qt_ref, m_ref, l_ref, acc_ref, *, n_ctx, nk):
    qi, kj = pl.program_id(1), pl.program_id(2)
    gw = ATTN_GROUP * HEAD_DIM

    @pl.when(kj == 0)
    def _():
        qt_ref[...] = jnp.zeros(qt_ref.shape, BF16)
        for g in range(ATTN_KV_HEADS):
            qg_t = q_ref[:, g * gw:(g + 1) * gw].astype(F32).T.astype(BF16)
            for h in range(ATTN_GROUP):
                qt_ref[g, g * HEAD_DIM:(g + 1) * HEAD_DIM, h * qg_t.shape[1]:(h + 1) * qg_t.shape[1]] = (
                    qg_t[h * HEAD_DIM:(h + 1) * HEAD_DIM, :])
        m_ref[...] = jnp.full(m_ref.shape, -jnp.inf, F32)
        l_ref[...] = jnp.zeros(l_ref.shape, F32)
        acc_ref[...] = jnp.zeros(acc_ref.shape, F32)

    @pl.when((qi == 0) & (kj == 0))
    def _():
        _attn_step(qt_ref, k_ref, vt_ref, m_ref, l_ref, acc_ref, n_ctx)

    @pl.when(qi > 0)
    def _():
        _attn_step(qt_ref, k_ref, vt_ref, m_ref, l_ref, acc_ref, None)

    @pl.when(kj == nk - 1)
    def _():
        tq = q_ref.shape[0]
        for g in range(ATTN_KV_HEADS):
            o = acc_ref[g] / l_ref[g]
            o = jnp.concatenate([o[:, h * tq:(h + 1) * tq] for h in range(ATTN_GROUP)], axis=0)
            o_ref[:, g * gw:(g + 1) * gw] = o.T.astype(BF16)


def _attention(q, k, vt, n_ctx):
    b, t, _ = q.shape
    tq = TOKEN_TILE
    tk = KV_TILE if t % KV_TILE == 0 else TOKEN_TILE
    nk = t // tk
    assert n_ctx == tq and n_ctx <= tk
    kblock = lambda i, j: jnp.where(i == 0, 0, j)
    return pl.pallas_call(
        functools.partial(_attn_kernel, n_ctx=n_ctx, nk=nk),
        grid=(b, t // tq, nk),
        in_specs=[pl.BlockSpec((None, tq, ATTN_DIM), lambda bb, i, j: (bb, i, 0)),
                  pl.BlockSpec((None, tk, KV_DIM), lambda bb, i, j: (bb, kblock(i, j), 0)),
                  pl.BlockSpec((None, KV_DIM, tk), lambda bb, i, j: (bb, 0, kblock(i, j)))],
        out_specs=pl.BlockSpec((None, tq, ATTN_DIM), lambda bb, i, j: (bb, i, 0)),
        out_shape=jax.ShapeDtypeStruct((b, t, ATTN_DIM), BF16),
        scratch_shapes=[pltpu.VMEM((ATTN_KV_HEADS, KV_DIM, ATTN_GROUP * tq), BF16),
                        pltpu.VMEM((ATTN_KV_HEADS, 1, ATTN_GROUP * tq), F32),
                        pltpu.VMEM((ATTN_KV_HEADS, 1, ATTN_GROUP * tq), F32),
                        pltpu.VMEM((ATTN_KV_HEADS, HEAD_DIM, ATTN_GROUP * tq), F32)],
        compiler_params=_params(("parallel", "parallel", "arbitrary")),
        name="attention",
    )(q, k, vt)


def _feat_kernel(rw_ref, prev_ref, next_ref, taps_ref, dbase_ref, dup_ref, ibase_ref, iup_ref, gup_ref,
                 kk_par_ref, ka_ref, rk_ref,
                 r_out, v_out, kk_out, gate_out, bonus_out, lw_out, kd_out, bb_out, *, nt):
    i = pl.program_id(1)
    x = rw_ref[...]
    tm = x.shape[0]
    prev_row = jnp.where(i <= 1, 0.0, prev_ref[7:8, :])
    next_row = jnp.where((i == 0) | (i == nt - 1), 0.0, next_ref[0:1, :])
    row = _iota(x.shape, 0)
    xm = jnp.where(row == 0, prev_row, pltpu.roll(x, 1, 0))
    xp = jnp.where(row == tm - 1, next_row, pltpu.roll(x, tm - 1, 0))
    xs = xm * taps_ref[0:1, :] + x * taps_ref[1:2, :] + xp * taps_ref[2:3, :]

    o3 = 3 * RWKV_DIM
    r, k, v = xs[:, :RWKV_DIM], xs[:, RWKV_DIM:2 * RWKV_DIM], xs[:, 2 * RWKV_DIM:o3]
    wd = jnp.tanh(xs[:, o3:o3 + LORA_COLS]).astype(BF16)
    ad = xs[:, o3 + LORA_COLS:o3 + 2 * LORA_COLS].astype(BF16)
    gd = jax.nn.sigmoid(xs[:, o3 + 2 * LORA_COLS:]).astype(BF16)
    ones = _head_ones(256)
    kkr = k * kk_par_ref[...]
    kk = kkr * lax.rsqrt(_head_sum(kkr * kkr, ones) + L2_EPS)
    r_out[...] = r
    v_out[...] = v
    kk_out[...] = kk
    gate_out[...] = jnp.dot(gd, gup_ref[...], preferred_element_type=F32)
    bonus_out[...] = _head_sum(r * k * rk_ref[...], ones) * v
    for d in range(N_DIRS):
        z = dbase_ref[d:d + 1, :] + jnp.dot(wd, dup_ref[d], preferred_element_type=F32)
        lw_out[d] = jax.nn.sigmoid(z) * (-math.exp(-0.5))
        iclr = jax.nn.sigmoid(ibase_ref[d:d + 1, :] + jnp.dot(ad, iup_ref[d], preferred_element_type=F32))
        kd_out[d] = k * (1.0 + (iclr - 1.0) * ka_ref[...])
        bb_out[d] = kk * iclr


def _rwkv_features(rw, taps, dbase, dup, ibase, iup, gup, k_k, k_a, r_k):
    b, t, w = rw.shape
    tm = TOKEN_TILE
    nt = t // tm
    r8 = tm // 8
    full = lambda a: pl.BlockSpec(a.shape, lambda bb, i: (0,) * a.ndim)
    tile = pl.BlockSpec((None, tm, RWKV_DIM), lambda bb, i: (bb, i, 0))
    dtile = pl.BlockSpec((None, N_DIRS, tm, RWKV_DIM), lambda bb, i: (bb, 0, i, 0))
    shared = jax.ShapeDtypeStruct((b, t, RWKV_DIM), F32)
    perdir = jax.ShapeDtypeStruct((b, N_DIRS, t, RWKV_DIM), F32)
    consts = (taps, dbase, dup, ibase, iup, gup, k_k, k_a, r_k)
    return pl.pallas_call(
        functools.partial(_feat_kernel, nt=nt),
        grid=(b, nt),
        in_specs=[pl.BlockSpec((None, tm, w), lambda bb, i: (bb, i, 0)),
                  pl.BlockSpec((None, 8, w), lambda bb, i: (bb, jnp.maximum(i * r8 - 1, 0), 0)),
                  pl.BlockSpec((None, 8, w), lambda bb, i: (bb, jnp.minimum((i + 1) * r8, nt * r8 - 1), 0))]
                 + [full(a) for a in consts],
        out_specs=[tile] * 5 + [dtile] * 3,
        out_shape=[shared] * 5 + [perdir] * 3,
        compiler_params=_params(("parallel", "parallel")),
        name="rwkv_features",
    )(rw, rw, rw, *consts)


def _scan_kernel(r_ref, v_ref, kk_ref, lw_ref, kd_ref, bb_ref, y_ref, st_ref):
    rev = pl.program_id(1) == 1
    c = pl.program_id(2)

    @pl.when(c == 0)
    def _():
        st_ref[...] = jnp.zeros(st_ref.shape, F32)

    ch, pw = CHUNK, LANES
    ti, si = _iota((ch, ch), 0), _iota((ch, ch), 1)
    tri = (jnp.where(rev, si - ti, ti - si) >= 0).astype(F32)
    a, bcol = _iota((pw, pw), 0), _iota((pw, pw), 1)
    same = (a >> 6) == (bcol >> 6)
    ahead = jnp.where(rev, a - bcol, bcol - a)
    strict = same & (ahead > 0)
    incl = same & (ahead >= 0)
    eye = a == bcol
    head0 = _iota((ch, pw), 1) < HEAD_DIM

    def stack(x):
        return jnp.concatenate([jnp.where(head0, x, 0.0), jnp.where(head0, 0.0, x)], axis=0)

    def dup(x):
        return jnp.concatenate([x, x], axis=0)

    def mm(x, y):
        return jnp.dot(x.astype(BF16), y.astype(BF16), preferred_element_type=F32)

    pairs = range(RWKV_DIM // pw)
    sls = [slice(p * pw, (p + 1) * pw) for p in pairs]
    each = lambda f, *cols: [f(*args) for args in zip(*cols)]
    lw = [lw_ref[:, sl] for sl in sls]
    cum = each(lambda a_: jnp.dot(tri, a_, preferred_element_type=F32, precision=HIGHEST), lw)
    tot = each(lambda a_: jnp.sum(a_, axis=0, keepdims=True), lw)
    w_inv = each(lambda c_: jnp.exp(-c_), cum)
    w_rem = each(lambda t_, c_: jnp.exp(t_ - c_), tot, cum)
    at = each(lambda sl, c_, l_: -kk_ref[:, sl] * jnp.exp(c_ - l_), sls, cum, lw)
    rt = each(lambda sl, c_: r_ref[:, sl] * jnp.exp(c_), sls, cum)
    bt = each(lambda sl, w_: bb_ref[:, sl] * w_, sls, w_inv)
    kt = each(lambda sl, w_: kd_ref[:, sl] * w_, sls, w_inv)
    bw = each(lambda sl, w_: bb_ref[:, sl] * w_, sls, w_rem)
    kw = each(lambda sl, w_: kd_ref[:, sl] * w_, sls, w_rem)

    nt_dims = (((1,), (1,)), ((), ()))
    m = each(lambda b_, k_, a_, r_: lax.dot_general(
        jnp.concatenate([stack(b_), stack(k_)], axis=0).astype(BF16),
        jnp.concatenate([dup(a_), dup(r_)], axis=0).astype(BF16), nt_dims, preferred_element_type=F32),
        bt, kt, at, rt)
    mab = each(lambda m_: jnp.where(strict, m_[:pw, :pw], 0.0), m)
    mbr = each(lambda m_: jnp.where(incl, m_[:pw, pw:], 0.0), m)
    mak = each(lambda m_: jnp.where(strict, m_[pw:, :pw], 0.0), m)
    mkr = each(lambda m_: jnp.where(incl, m_[pw:, pw:], 0.0), m)

    n = mab
    x = each(lambda b_, m_: jnp.concatenate([stack(b_), m_], axis=1), bw, mbr)
    levels = CHUNK.bit_length() - 1
    for lvl in range(levels - 1):
        nx = each(lambda n_, x_: mm(n_, jnp.concatenate([n_, x_], axis=1)), n, x)
        n = [nx_[:, :pw] for nx_ in nx]
        x = each(lambda x_, nx_: x_ + nx_[:, pw:], x, nx)
    x = each(lambda n_, x_: x_ + mm(n_, x_), n, x)
    z = [x_[:, :pw] for x_ in x]
    tm_ = [x_[:, pw:] for x_ in x]

    a2 = each(stack, at)
    v2 = each(lambda sl: stack(v_ref[:, sl]), sls)
    pt = each(lambda z_, a_, t_: mm(z_.T, a_) + jnp.where(eye, jnp.exp(t_), 0.0), z, a2, tot)
    dd = each(lambda k_, z_, w_: mm(k_, z_) + stack(w_), mak, z, kw)
    qt = each(lambda d_, v_: mm(d_.T, v_), dd, v2)
    g = each(lambda r_, t_, a_: stack(r_) + mm(t_.T, a_), rt, tm_, a2)
    hm = each(lambda kr_, k_, t_: kr_ + mm(k_, t_), mkr, mak, tm_)
    hv = each(lambda h_, v_: mm(h_.T, v_), hm, v2)
    st = [st_ref[p] for p in pairs]
    ys = each(lambda g_, s_, h_: mm(g_, s_) + h_, g, st, hv)
    new = each(lambda p_, s_, q_: mm(p_, s_) + q_, pt, st, qt)
    for p in pairs:
        y_ref[:, sls[p]] = ys[p][:ch] + ys[p][ch:]
        st_ref[p] = new[p]


def _rwkv_scan(r, v, kk, lw, kd, bb, n_ctx):
    b, t, w = r.shape
    nc, ncx = t // CHUNK, n_ctx // CHUNK

    def chunk(d, c):
        return jnp.where(d == 0, c, jnp.where(c < ncx, ncx - 1 - c, nc - 1 + ncx - c))

    shared = pl.BlockSpec((None, CHUNK, w), lambda bb_, d, c: (bb_, chunk(d, c), 0))
    perdir = pl.BlockSpec((None, None, CHUNK, w), lambda bb_, d, c: (bb_, d, chunk(d, c), 0))
    return pl.pallas_call(
        _scan_kernel,
        grid=(b, N_DIRS, nc),
        in_specs=[shared] * 3 + [perdir] * 3,
        out_specs=perdir,
        out_shape=jax.ShapeDtypeStruct((b, N_DIRS, t, w), F32),
        scratch_shapes=[pltpu.VMEM((w // LANES, LANES, LANES), F32)],
        compiler_params=_params(("parallel", "parallel", "arbitrary")),
        name="rwkv_scan",
    )(r, v, kk, lw, kd, bb)


def _out_kernel(y0_ref, y1_ref, bonus_ref, gate_ref, attn_ref, x_ref, lnw_ref, lnb_ref, w_ref, g_ref, o_ref):
    ones = _head_ones(256)
    y = y0_ref[...] + y1_ref[...]
    yc = y - _head_sum(y, ones) * (1.0 / HEAD_DIM)
    var = _head_sum(yc * yc, ones) * (1.0 / HEAD_DIM)
    yn = yc * lax.rsqrt(var + GN_EPS) * lnw_ref[...] + lnb_ref[...]
    rw = ((yn + bonus_ref[...]) * gate_ref[...]).astype(BF16)
    mix = jnp.dot(jnp.concatenate([attn_ref[...], rw], axis=1), w_ref[...], preferred_element_type=F32)
    o_ref[...] = x_ref[...] + g_ref[...] * mix


def _mixer_out(y, bonus, gate, attn, xs, ln_w, ln_b, w_out, mod):
    b, t, d = xs.shape
    tm = TOKEN_TILE
    tile = lambda w: pl.BlockSpec((None, tm, w), lambda bb, i: (bb, i, 0))
    ydir = lambda dd: pl.BlockSpec((None, None, tm, RWKV_DIM), lambda bb, i: (bb, dd, i, 0))
    row = lambda w: pl.BlockSpec((1, w), lambda bb, i: (0, 0))
    return pl.pallas_call(
        _out_kernel,
        grid=(b, t // tm),
        in_specs=[ydir(0), ydir(1), tile(RWKV_DIM), tile(RWKV_DIM), tile(ATTN_DIM), tile(d),
                  row(RWKV_DIM), row(RWKV_DIM), pl.BlockSpec(w_out.shape, lambda bb, i: (0, 0)),
                  _mod_spec(b, 2, d)],
        out_specs=tile(d),
        out_shape=jax.ShapeDtypeStruct((b, t, d), F32),
        compiler_params=_params(("parallel", "parallel")),
        name="mixer_out",
    )(y, y, bonus, gate, attn, xs, ln_w, ln_b, w_out, mod)


def _query_kernel(x_ref, gain_ref, sh_ref, sc_ref, w_ref, k1_ref, k2_ref, ht_ref, s1_ref, s2_ref):
    hf = _norm_mod(x_ref[...], gain_ref[...], sh_ref[...], sc_ref[...])
    ht_ref[...] = hf.T.astype(BF16)
    q = jnp.dot(hf.astype(BF16), w_ref[...], preferred_element_type=F32).astype(BF16)
    nt = (((1,), (1,)), ((), ()))
    for hd in range(PEER_HEADS):
        c0 = hd * 2 * PEER_HALF
        s1_ref[hd] = lax.dot_general(k1_ref[...], q[:, c0:c0 + PEER_HALF], nt, preferred_element_type=F32)
        s2_ref[hd] = lax.dot_general(k2_ref[...], q[:, c0 + PEER_HALF:c0 + 2 * PEER_HALF], nt,
                                     preferred_element_type=F32)


def _peer_scores(xs, mod, gain, w_query, keys1, keys2):
    b, t, d = xs.shape
    tm = TOKEN_TILE
    nt = t // tm
    row = lambda w: pl.BlockSpec((1, w), lambda bb, i: (0, 0))
    full = lambda a: pl.BlockSpec(a.shape, lambda bb, i: (0,) * a.ndim)
    sspec = pl.BlockSpec((PEER_HEADS, N_KEYS, tm), lambda bb, i: (0, 0, bb * nt + i))
    sshape = jax.ShapeDtypeStruct((PEER_HEADS, N_KEYS, b * t), F32)
    return pl.pallas_call(
        _query_kernel,
        grid=(b, nt),
        in_specs=[pl.BlockSpec((None, tm, d), lambda bb, i: (bb, i, 0)), row(d), _mod_spec(b, 3, d),
                  _mod_spec(b, 4, d), full(w_query), full(keys1), full(keys2)],
        out_specs=[pl.BlockSpec((d, tm), lambda bb, i: (0, bb * nt + i)), sspec, sspec],
        out_shape=[jax.ShapeDtypeStruct((d, b * t), BF16), sshape, sshape],
        compiler_params=_params(("parallel", "parallel")),
        name="peer_scores",
    )(xs, gain, mod, mod, w_query, keys1, keys2)


def _top_rows(x_ref, out_ref):
    slabs, n, tl = x_ref.shape
    idx = _iota((n, tl), 0).astype(F32)

    def body(i, carry):
        xs = [x_ref[s] for s in range(slabs)]
        ms = [jnp.max(x, axis=0, keepdims=True) for x in xs]
        firsts = [jnp.min(jnp.where(x == m, idx, float(n)), axis=0, keepdims=True) for x, m in zip(xs, ms)]
        for s in range(slabs):
            out_ref[s, pl.ds(i, 1), :] = ms[s]
            x_ref[s] = jnp.where(idx == firsts[s], -jnp.inf, xs[s])
        return carry

    lax.fori_loop(0, PEER_TOPK, body, 0)


_CAND_COUNTS = [PEER_TOPK // (i + 1) for i in range(PEER_TOPK)]
_CAND_ROWS = -(-sum(_CAND_COUNTS) // 8) * 8


def _topk_kernel(s1_ref, s2_ref, thr_ref, lse_ref, work_ref, v_ref, cand_ref, top_ref):
    work_ref[:PEER_HEADS] = s1_ref[...]
    work_ref[PEER_HEADS:] = s2_ref[...]
    _top_rows(work_ref, v_ref)
    cand_ref[...] = jnp.full(cand_ref.shape, -jnp.inf, F32)
    for hd in range(PEER_HEADS):
        off = 0
        for i, cnt in enumerate(_CAND_COUNTS):
            cand_ref[hd, off:off + cnt, :] = v_ref[hd, i:i + 1, :] + v_ref[PEER_HEADS + hd, 0:cnt, :]
            off += cnt
    _top_rows(cand_ref, top_ref)
    for hd in range(PEER_HEADS):
        top = top_ref[hd]
        mx = top[0:1, :]
        thr_ref[hd:hd + 1, :] = top[PEER_TOPK - 1:PEER_TOPK, :]
        lse_ref[hd:hd + 1, :] = mx + jnp.log(jnp.sum(jnp.exp(top - mx), axis=0, keepdims=True))


def _peer_topk(s1, s2):
    _, _, n = s1.shape
    tl = TOKEN_TILE
    sspec = pl.BlockSpec((PEER_HEADS, N_KEYS, tl), lambda i: (0, 0, i))
    ospec = pl.BlockSpec((PEER_HEADS, tl), lambda i: (0, i))
    oshape = jax.ShapeDtypeStruct((PEER_HEADS, n), F32)
    return pl.pallas_call(
        _topk_kernel,
        grid=(n // tl,),
        in_specs=[sspec, sspec],
        out_specs=[ospec, ospec],
        out_shape=[oshape, oshape],
        scratch_shapes=[pltpu.VMEM((2 * PEER_HEADS, N_KEYS, tl), F32),
                        pltpu.VMEM((2 * PEER_HEADS, PEER_TOPK, tl), F32),
                        pltpu.VMEM((PEER_HEADS, _CAND_ROWS, tl), F32),
                        pltpu.VMEM((PEER_HEADS, PEER_TOPK, tl), F32)],
        compiler_params=_params(("parallel",)),
        name="peer_topk",
    )(s1, s2)


def _peer_kernel(ht_ref, s1_ref, s2_ref, thr_ref, lse_ref, u_ref, vt_ref, x_ref, g_ref, o_ref, acc_ref, *, n_steps):
    ap = pl.program_id(2)

    @pl.when(ap == 0)
    def _():
        acc_ref[...] = jnp.zeros(acc_ref.shape, F32)

    thr = [thr_ref[hd:hd + 1, :] for hd in range(PEER_HEADS)]
    lse = [lse_ref[hd:hd + 1, :] for hd in range(PEER_HEADS)]
    ht = ht_ref[...]
    sub, rows = EXPERT_SUB, EXPERT_SUB // N_KEYS

    def pre_act(j):
        return jnp.dot(u_ref[j * sub:(j + 1) * sub, :], ht, preferred_element_type=F32)

    pre, upd = pre_act(0), None
    for j in range(EXPERT_TILE // sub):
        nxt = pre_act(j + 1) if (j + 1) * sub < EXPERT_TILE else None
        gates = []
        for r in range(rows):
            a = ap * (EXPERT_TILE // N_KEYS) + j * rows + r
            w = None
            for hd in range(PEER_HEADS):
                pair = s1_ref[hd, pl.ds(a, 1), :] + s2_ref[hd]
                e = jnp.where(pair >= thr[hd], jnp.exp(pair - lse[hd]), 0.0)
                w = e if w is None else w + e
            gates.append(w)
        act = 0.5 * pre * (1.0 + lax.erf(pre * (2.0 ** -0.5)))
        gw = (jnp.concatenate(gates, axis=0) * act).astype(BF16)
        part = jnp.dot(vt_ref[:, j * sub:(j + 1) * sub], gw, preferred_element_type=F32)
        upd = part if upd is None else upd + part
        pre = nxt
    acc_ref[...] += upd

    @pl.when(ap == n_steps - 1)
    def _():
        o_ref[...] = x_ref[...] + g_ref[...] * acc_ref[...].T


def _peer_experts(ht, s1, s2, thr, lse, u, vt, xs, mod):
    b, t, d = xs.shape
    tm = TOKEN_TILE
    nt = t // tm
    n_steps = u.shape[0] // EXPERT_TILE
    tok = lambda bb, i, a: bb * nt + i
    sspec = pl.BlockSpec((PEER_HEADS, N_KEYS, tm), lambda bb, i, a: (0, 0, tok(bb, i, a)))
    rspec = pl.BlockSpec((PEER_HEADS, tm), lambda bb, i, a: (0, tok(bb, i, a)))
    tile = pl.BlockSpec((None, tm, d), lambda bb, i, a: (bb, i, 0))
    return pl.pallas_call(
        functools.partial(_peer_kernel, n_steps=n_steps),
        grid=(b, nt, n_steps),
        in_specs=[pl.BlockSpec((d, tm), lambda bb, i, a: (0, tok(bb, i, a))), sspec, sspec, rspec, rspec,
                  pl.BlockSpec((EXPERT_TILE, d), lambda bb, i, a: (a, 0)),
                  pl.BlockSpec((d, EXPERT_TILE), lambda bb, i, a: (0, a)),
                  tile, _mod_spec(b, 5, d)],
        out_specs=tile,
        out_shape=jax.ShapeDtypeStruct((b, t, d), F32),
        scratch_shapes=[pltpu.VMEM((d, tm), F32)],
        compiler_params=_params(("parallel", "parallel", "arbitrary")),
        name="peer_experts",
    )(ht, s1, s2, thr, lse, u, vt, xs, mod)


def _rope_tables(n_lat, n_ctx):
    pos = jnp.arange(n_lat)
    inv_freq = ROPE_THETA ** (-jnp.arange(AXIS_FREQS, dtype=F32) * 2.0 / AXIS_ROT_DIM)
    ang = jnp.stack([(pos // GRID_W).astype(F32)[:, None] * inv_freq,
                     (pos % GRID_W).astype(F32)[:, None] * inv_freq], axis=1)
    cos = jnp.broadcast_to(jnp.cos(ang)[:, :, None, :], (n_lat, 2, 2, AXIS_FREQS)).reshape(n_lat, HEAD_DIM)
    sin = jnp.stack([-jnp.sin(ang), jnp.sin(ang)], axis=2).reshape(n_lat, HEAD_DIM)
    cos = jnp.concatenate([jnp.ones((n_ctx, HEAD_DIM), F32), cos], axis=0)
    sin = jnp.concatenate([jnp.zeros((n_ctx, HEAD_DIM), F32), sin], axis=0)
    return jnp.tile(cos, (1, LANES // HEAD_DIM)), jnp.tile(sin, (1, LANES // HEAD_DIM))


def _pad_lora(up):
    z = jnp.zeros_like(up[0])
    return jnp.stack([jnp.concatenate([up[0], z], axis=0), jnp.concatenate([z, up[1]], axis=0)]).astype(BF16)


def kernel(x, c, ctx, c_ctx, mod_w, mod_b, norm_mix, norm_ffn, w_in, q_gain, k_gain, shift_taps, decay_base,
           decay_up, iclr_base, iclr_up, gate_up, k_k, k_a, r_k, ln_x_w, ln_x_b, w_out, peer_query, peer_subkeys1,
           peer_subkeys2, expert_u, expert_v):
    b, n_lat, d = x.shape
    n_ctx = ctx.shape[1]
    depth = mod_w.shape[0]
    assert n_ctx == TOKEN_TILE and n_lat % TOKEN_TILE == 0 and b < 8

    xs = jnp.concatenate([ctx, x], axis=1)
    cc = jnp.concatenate([c, c_ctx[None, :], jnp.zeros((8 - b - 1, d), F32)], axis=0)
    mod_all = _modulation(cc, mod_w, mod_b)
    cos_t, sin_t = _rope_tables(n_lat, n_ctx)
    row = lambda a: a.reshape(1, -1)

    for l in range(depth):
        mod = mod_all[l].reshape(8, 1, N_MOD * d)
        q, k, v, rw = _in_proj(xs, mod, row(norm_mix[l]), w_in[l].astype(BF16),
                               row(jnp.tile(q_gain[l], ATTN_HEADS)), row(jnp.tile(k_gain[l], ATTN_KV_HEADS)),
                               cos_t, sin_t)
        attn = _attention(q, k, v, n_ctx)
        r, vv, kk, gate, bonus, lw, kd, bb = _rwkv_features(
            rw, shift_taps[l], decay_base[l], _pad_lora(decay_up[l]), iclr_base[l], _pad_lora(iclr_up[l]),
            gate_up[l].astype(BF16), row(k_k[l]), row(k_a[l]), row(r_k[l]))
        y = _rwkv_scan(r, vv, kk, lw, kd, bb, n_ctx)
        xs = _mixer_out(y, bonus, gate, attn, xs, row(ln_x_w[l]), row(ln_x_b[l]), w_out[l].astype(BF16), mod)
        ht, s1, s2 = _peer_scores(xs, mod, row(norm_ffn[l]), peer_query[l].astype(BF16),
                                  peer_subkeys1[l].astype(BF16), peer_subkeys2[l].astype(BF16))
        thr, lse = _peer_topk(s1, s2)
        xs = _peer_experts(ht, s1, s2, thr, lse, expert_u[l].astype(BF16), expert_v[l].T.astype(BF16), xs, mod)
    return xs[:, n_ctx:, :]
```

```python
import functools
import math

import jax
import jax.numpy as jnp
from jax import lax
from jax.experimental import pallas as pl
from jax.experimental.pallas import tpu as pltpu

F32, BF16 = jnp.float32, jnp.bfloat16
HIGHEST = lax.Precision.HIGHEST

HEAD_DIM = 64
ATTN_HEADS = 8
ATTN_KV_HEADS = 2
ATTN_GROUP = ATTN_HEADS // ATTN_KV_HEADS
ATTN_DIM = ATTN_HEADS * HEAD_DIM
KV_DIM = ATTN_KV_HEADS * HEAD_DIM
ATTN_COLS = ATTN_DIM + 2 * KV_DIM
GRID_W = 64
ROPE_THETA = 10000.0
AXIS_ROT_DIM = HEAD_DIM // 2
AXIS_FREQS = AXIS_ROT_DIM // 2
RWKV_DIM = 512
LORA_COLS = 128
N_DIRS = 2
RWKV_COLS = 3 * RWKV_DIM + 3 * LORA_COLS
N_KEYS = 128
PEER_HEADS = 8
PEER_HALF = 128
PEER_TOPK = 16
N_MOD = 6
NORM_EPS = 1e-6
GN_EPS = 64e-5
L2_EPS = 1e-12

LANES = 128
TOKEN_TILE = 256
KV_TILE = 1408
CHUNK = 64
EXPERT_TILE = 2048
EXPERT_SUB = 256
VMEM_LIMIT = 48 * 1024 * 1024


def _params(sem):
    return pltpu.CompilerParams(dimension_semantics=sem, vmem_limit_bytes=VMEM_LIMIT)


def _iota(shape, axis):
    return lax.broadcasted_iota(jnp.int32, shape, axis)


def _head_ones(n):
    return ((_iota((n, n), 0) >> 6) == (_iota((n, n), 1) >> 6)).astype(BF16)


def _head_sum(x, ones):
    n = ones.shape[0]
    outs = []
    for c0 in range(0, x.shape[1], n):
        rest = x[:, c0:c0 + n]
        acc = None
        for _ in range(3):
            part = rest.astype(BF16)
            rest = rest - part.astype(F32)
            t = jnp.dot(part, ones, preferred_element_type=F32)
            acc = t if acc is None else acc + t
        outs.append(acc)
    return outs[0] if len(outs) == 1 else jnp.concatenate(outs, axis=1)


def _norm_mod(x, gain, shift, scale):
    ms = jnp.mean(x * x, axis=-1, keepdims=True)
    return (x * lax.rsqrt(ms + NORM_EPS) * gain) * (1.0 + scale) + shift


def _mod_kernel(c_ref, w_ref, b_ref, o_ref):
    c = c_ref[...]
    a = c * jax.nn.sigmoid(c)
    o_ref[...] = jnp.dot(a, w_ref[...], preferred_element_type=F32, precision=HIGHEST) + b_ref[...]


def _modulation(cc, mod_w, mod_b):
    depth, d, n = mod_w.shape
    return pl.pallas_call(
        _mod_kernel,
        grid=(depth, n // d),
        in_specs=[pl.BlockSpec((8, d), lambda l, j: (0, 0)),
                  pl.BlockSpec((None, d, d), lambda l, j: (l, 0, j)),
                  pl.BlockSpec((None, 1, d), lambda l, j: (l, 0, j))],
        out_specs=pl.BlockSpec((None, 8, d), lambda l, j: (l, 0, j)),
        out_shape=jax.ShapeDtypeStruct((depth, 8, n), F32),
        compiler_params=_params(("parallel", "parallel")),
        name="modulation",
    )(cc, mod_w, mod_b.reshape(depth, 1, n))


def _mod_spec(batch, chunk, d):
    return pl.BlockSpec((None, 1, d), lambda b, i, *_: (jnp.where(i == 0, batch, b), 0, chunk))


def _rope(x, cos, sin):
    w = x.shape[1]
    reps = w // LANES
    if reps > 1:
        cos = jnp.concatenate([cos] * reps, axis=1)
        sin = jnp.concatenate([sin] * reps, axis=1)
    first = (_iota(x.shape, 1) & AXIS_FREQS) == 0
    partner = jnp.where(first, pltpu.roll(x, w - AXIS_FREQS, 1), pltpu.roll(x, AXIS_FREQS, 1))
    return x * cos + partner * sin


def _head_norm(x, gain, ones):
    ms = _head_sum(x * x, ones) * (1.0 / HEAD_DIM)
    return x * lax.rsqrt(ms + NORM_EPS) * gain


def _in_kernel(x_ref, gain_ref, sh_ref, sc_ref, w_ref, qg_ref, kg_ref, cos_ref, sin_ref,
               q_ref, k_ref, vt_ref, rw_ref):
    h = _norm_mod(x_ref[...], gain_ref[...], sh_ref[...], sc_ref[...])
    p = jnp.dot(h.astype(BF16), w_ref[...], preferred_element_type=F32)
    cos, sin = cos_ref[...], sin_ref[...]
    q = _head_norm(p[:, :ATTN_DIM], qg_ref[...], _head_ones(256))
    q_ref[...] = (_rope(q, cos, sin) * (HEAD_DIM ** -0.5)).astype(BF16)
    k = _head_norm(p[:, ATTN_DIM:ATTN_DIM + KV_DIM], kg_ref[...], _head_ones(KV_DIM))
    k_ref[...] = _rope(k, cos, sin).astype(BF16)
    vt_ref[...] = p[:, ATTN_DIM + KV_DIM:ATTN_COLS].T.astype(BF16)
    rw_ref[...] = p[:, ATTN_COLS:]


def _in_proj(xs, mod, gain, w_in, q_gain, k_gain, cos_t, sin_t):
    b, t, d = xs.shape
    tm = TOKEN_TILE
    n = w_in.shape[1]
    tile = lambda w: pl.BlockSpec((None, tm, w), lambda bb, i: (bb, i, 0))
    row = lambda w: pl.BlockSpec((1, w), lambda bb, i: (0, 0))
    return pl.pallas_call(
        _in_kernel,
        grid=(b, t // tm),
        in_specs=[tile(d), row(d), _mod_spec(b, 0, d), _mod_spec(b, 1, d),
                  pl.BlockSpec((d, n), lambda bb, i: (0, 0)), row(ATTN_DIM), row(KV_DIM),
                  pl.BlockSpec((tm, LANES), lambda bb, i: (i, 0)),
                  pl.BlockSpec((tm, LANES), lambda bb, i: (i, 0))],
        out_specs=[tile(ATTN_DIM), tile(KV_DIM), pl.BlockSpec((None, KV_DIM, tm), lambda bb, i: (bb, 0, i)),
                   tile(RWKV_COLS)],
        out_shape=[jax.ShapeDtypeStruct((b, t, ATTN_DIM), BF16),
                   jax.ShapeDtypeStruct((b, t, KV_DIM), BF16),
                   jax.ShapeDtypeStruct((b, KV_DIM, t), BF16),
                   jax.ShapeDtypeStruct((b, t, RWKV_COLS), F32)],
        compiler_params=_params(("parallel", "parallel")),
        name="in_proj",
    )(xs, gain, mod, mod, w_in, q_gain, k_gain, cos_t, sin_t)


def _attn_step(qt_ref, k_ref, vt_ref, m_ref, l_ref, acc_ref, n_valid):
    k = k_ref[...]
    groups = range(ATTN_KV_HEADS)
    s = [jnp.dot(k, qt_ref[g], preferred_element_type=F32) for g in groups]
    if n_valid is not None:
        s = [jnp.where(_iota(sg.shape, 0) < n_valid, sg, -jnp.inf) for sg in s]
    for g in groups:
        m_old = m_ref[g]
        m_new = jnp.maximum(m_old, jnp.max(s[g], axis=0, keepdims=True))
        alpha = jnp.exp(m_old - m_new)
        p = jnp.exp(s[g] - m_new)
        l_ref[g] = alpha * l_ref[g] + jnp.sum(p, axis=0, keepdims=True)
        pv = jnp.dot(vt_ref[g * HEAD_DIM:(g + 1) * HEAD_DIM, :], p.astype(BF16), preferred_element_type=F32)
        acc_ref[g] = alpha * acc_ref[g] + pv
        m_ref[g] = m_new


def _attn_kernel(q_ref, k_ref, vt_ref, o_ref, qt_ref, m_ref, l_ref, acc_ref, *, n_ctx, nk):
    qi, kj = pl.program_id(1), pl.program_id(2)
    gw = ATTN_GROUP * HEAD_DIM

    @pl.when(kj == 0)
    def _():
        qt_ref[...] = jnp.zeros(qt_ref.shape, BF16)
        for g in range(ATTN_KV_HEADS):
            qg_t = q_ref[:, g * gw:(g + 1) * gw].astype(F32).T.astype(BF16)
            for h in range(ATTN_GROUP):
                qt_ref[g, g * HEAD_DIM:(g + 1) * HEAD_DIM, h * qg_t.shape[1]:(h + 1) * qg_t.shape[1]] = (
                    qg_t[h * HEAD_DIM:(h + 1) * HEAD_DIM, :])
        m_ref[...] = jnp.full(m_ref.shape, -jnp.inf, F32)
        l_ref[...] = jnp.zeros(l_ref.shape, F32)
        acc_ref[...] = jnp.zeros(acc_ref.shape, F32)

    @pl.when((qi == 0) & (kj == 0))
    def _():
        _attn_step(qt_ref, k_ref, vt_ref, m_ref, l_ref, acc_ref, n_ctx)

    @pl.when(qi > 0)
    def _():
        _attn_step(qt_ref, k_ref, vt_ref, m_ref, l_ref, acc_ref, None)

    @pl.when(kj == nk - 1)
    def _():
        tq = q_ref.shape[0]
        for g in range(ATTN_KV_HEADS):
            o = acc_ref[g] / l_ref[g]
            o = jnp.concatenate([o[:, h * tq:(h + 1) * tq] for h in range(ATTN_GROUP)], axis=0)
            o_ref[:, g * gw:(g + 1) * gw] = o.T.astype(BF16)


def _attention(q, k, vt, n_ctx):
    b, t, _ = q.shape
    tq = TOKEN_TILE
    tk = KV_TILE if t % KV_TILE == 0 else TOKEN_TILE
    nk = t // tk
    assert n_ctx == tq and n_ctx <= tk
    kblock = lambda i, j: jnp.where(i == 0, 0, j)
    return pl.pallas_call(
        functools.partial(_attn_kernel, n_ctx=n_ctx, nk=nk),
        grid=(b, t // tq, nk),
        in_specs=[pl.BlockSpec((None, tq, ATTN_DIM), lambda bb, i, j: (bb, i, 0)),
                  pl.BlockSpec((None, tk, KV_DIM), lambda bb, i, j: (bb, kblock(i, j), 0)),
                  pl.BlockSpec((None, KV_DIM, tk), lambda bb, i, j: (bb, 0, kblock(i, j)))],
        out_specs=pl.BlockSpec((None, tq, ATTN_DIM), lambda bb, i, j: (bb, i, 0)),
        out_shape=jax.ShapeDtypeStruct((b, t, ATTN_DIM), BF16),
        scratch_shapes=[pltpu.VMEM((ATTN_KV_HEADS, KV_DIM, ATTN_GROUP * tq), BF16),
                        pltpu.VMEM((ATTN_KV_HEADS, 1, ATTN_GROUP * tq), F32),
                        pltpu.VMEM((ATTN_KV_HEADS, 1, ATTN_GROUP * tq), F32),
                        pltpu.VMEM((ATTN_KV_HEADS, HEAD_DIM, ATTN_GROUP * tq), F32)],
        compiler_params=_params(("parallel", "parallel", "arbitrary")),
        name="attention",
    )(q, k, vt)


def _feat_kernel(rw_ref, prev_ref, next_ref, taps_ref, dbase_ref, dup_ref, ibase_ref, iup_ref, gup_ref,
                 kk_par_ref, ka_ref, rk_ref,
                 r_out, v_out, kk_out, gate_out, bonus_out, lw_out, kd_out, bb_out, *, nt):
    i = pl.program_id(1)
    x = rw_ref[...]
    tm = x.shape[0]
    prev_row = jnp.where(i <= 1, 0.0, prev_ref[7:8, :])
    next_row = jnp.where((i == 0) | (i == nt - 1), 0.0, next_ref[0:1, :])
    row = _iota(x.shape, 0)
    xm = jnp.where(row == 0, prev_row, pltpu.roll(x, 1, 0))
    xp = jnp.where(row == tm - 1, next_row, pltpu.roll(x, tm - 1, 0))
    xs = xm * taps_ref[0:1, :] + x * taps_ref[1:2, :] + xp * taps_ref[2:3, :]

    o3 = 3 * RWKV_DIM
    r, k, v = xs[:, :RWKV_DIM], xs[:, RWKV_DIM:2 * RWKV_DIM], xs[:, 2 * RWKV_DIM:o3]
    wd = jnp.tanh(xs[:, o3:o3 + LORA_COLS]).astype(BF16)
    ad = xs[:, o3 + LORA_COLS:o3 + 2 * LORA_COLS].astype(BF16)
    gd = jax.nn.sigmoid(xs[:, o3 + 2 * LORA_COLS:]).astype(BF16)
    ones = _head_ones(256)
    kkr = k * kk_par_ref[...]
    kk = kkr * lax.rsqrt(_head_sum(kkr * kkr, ones) + L2_EPS)
    r_out[...] = r
    v_out[...] = v
    kk_out[...] = kk
    gate_out[...] = jnp.dot(gd, gup_ref[...], preferred_element_type=F32)
    bonus_out[...] = _head_sum(r * k * rk_ref[...], ones) * v
    for d in range(N_DIRS):
        z = dbase_ref[d:d + 1, :] + jnp.dot(wd, dup_ref[d], preferred_element_type=F32)
        lw_out[d] = jax.nn.sigmoid(z) * (-math.exp(-0.5))
        iclr = jax.nn.sigmoid(ibase_ref[d:d + 1, :] + jnp.dot(ad, iup_ref[d], preferred_element_type=F32))
        kd_out[d] = k * (1.0 + (iclr - 1.0) * ka_ref[...])
        bb_out[d] = kk * iclr


def _rwkv_features(rw, taps, dbase, dup, ibase, iup, gup, k_k, k_a, r_k):
    b, t, w = rw.shape
    tm = TOKEN_TILE
    nt = t // tm
    r8 = tm // 8
    full = lambda a: pl.BlockSpec(a.shape, lambda bb, i: (0,) * a.ndim)
    tile = pl.BlockSpec((None, tm, RWKV_DIM), lambda bb, i: (bb, i, 0))
    dtile = pl.BlockSpec((None, N_DIRS, tm, RWKV_DIM), lambda bb, i: (bb, 0, i, 0))
    shared = jax.ShapeDtypeStruct((b, t, RWKV_DIM), F32)
    perdir = jax.ShapeDtypeStruct((b, N_DIRS, t, RWKV_DIM), F32)
    consts = (taps, dbase, dup, ibase, iup, gup, k_k, k_a, r_k)
    return pl.pallas_call(
        functools.partial(_feat_kernel, nt=nt),
        grid=(b, nt),
        in_specs=[pl.BlockSpec((None, tm, w), lambda bb, i: (bb, i, 0)),
                  pl.BlockSpec((None, 8, w), lambda bb, i: (bb, jnp.maximum(i * r8 - 1, 0), 0)),
                  pl.BlockSpec((None, 8, w), lambda bb, i: (bb, jnp.minimum((i + 1) * r8, nt * r8 - 1), 0))]
                 + [full(a) for a in consts],
        out_specs=[tile] * 5 + [dtile] * 3,
        out_shape=[shared] * 5 + [perdir] * 3,
        compiler_params=_params(("parallel", "parallel")),
        name="rwkv_features",
    )(rw, rw, rw, *consts)


def _scan_kernel(r_ref, v_ref, kk_ref, lw_ref, kd_ref, bb_ref, y_ref, st_ref):
    rev = pl.program_id(1) == 1
    c = pl.program_id(2)

    @pl.when(c == 0)
    def _():
        st_ref[...] = jnp.zeros(st_ref.shape, F32)

    ch, pw = CHUNK, LANES
    ti, si = _iota((ch, ch), 0), _iota((ch, ch), 1)
    tri = (jnp.where(rev, si - ti, ti - si) >= 0).astype(F32)
    a, bcol = _iota((pw, pw), 0), _iota((pw, pw), 1)
    same = (a >> 6) == (bcol >> 6)
    ahead = jnp.where(rev, a - bcol, bcol - a)
    strict = same & (ahead > 0)
    incl = same & (ahead >= 0)
    eye = a == bcol
    head0 = _iota((ch, pw), 1) < HEAD_DIM

    def stack(x):
        return jnp.concatenate([jnp.where(head0, x, 0.0), jnp.where(head0, 0.0, x)], axis=0)

    def dup(x):
        return jnp.concatenate([x, x], axis=0)

    def mm(x, y):
        return jnp.dot(x.astype(BF16), y.astype(BF16), preferred_element_type=F32)

    pairs = range(RWKV_DIM // pw)
    sls = [slice(p * pw, (p + 1) * pw) for p in pairs]
    each = lambda f, *cols: [f(*args) for args in zip(*cols)]
    lw = [lw_ref[:, sl] for sl in sls]
    cum = each(lambda a_: jnp.dot(tri, a_, preferred_element_type=F32, precision=HIGHEST), lw)
    tot = each(lambda a_: jnp.sum(a_, axis=0, keepdims=True), lw)
    w_inv = each(lambda c_: jnp.exp(-c_), cum)
    w_rem = each(lambda t_, c_: jnp.exp(t_ - c_), tot, cum)
    at = each(lambda sl, c_, l_: -kk_ref[:, sl] * jnp.exp(c_ - l_), sls, cum, lw)
    rt = each(lambda sl, c_: r_ref[:, sl] * jnp.exp(c_), sls, cum)
    bt = each(lambda sl, w_: bb_ref[:, sl] * w_, sls, w_inv)
    kt = each(lambda sl, w_: kd_ref[:, sl] * w_, sls, w_inv)
    bw = each(lambda sl, w_: bb_ref[:, sl] * w_, sls, w_rem)
    kw = each(lambda sl, w_: kd_ref[:, sl] * w_, sls, w_rem)

    nt_dims = (((1,), (1,)), ((), ()))
    m = each(lambda b_, k_, a_, r_: lax.dot_general(
        jnp.concatenate([stack(b_), stack(k_)], axis=0).astype(BF16),
        jnp.concatenate([dup(a_), dup(r_)], axis=0).astype(BF16), nt_dims, preferred_element_type=F32),
        bt, kt, at, rt)
    mab = each(lambda m_: jnp.where(strict, m_[:pw, :pw], 0.0), m)
    mbr = each(lambda m_: jnp.where(incl, m_[:pw, pw:], 0.0), m)
    mak = each(lambda m_: jnp.where(strict, m_[pw:, :pw], 0.0), m)
    mkr = each(lambda m_: jnp.where(incl, m_[pw:, pw:], 0.0), m)

    n = mab
    x = each(lambda b_, m_: jnp.concatenate([stack(b_), m_], axis=1), bw, mbr)
    levels = CHUNK.bit_length() - 1
    for lvl in range(levels - 1):
        nx = each(lambda n_, x_: mm(n_, jnp.concatenate([n_, x_], axis=1)), n, x)
        n = [nx_[:, :pw] for nx_ in nx]
        x = each(lambda x_, nx_: x_ + nx_[:, pw:], x, nx)
    x = each(lambda n_, x_: x_ + mm(n_, x_), n, x)
    z = [x_[:, :pw] for x_ in x]
    tm_ = [x_[:, pw:] for x_ in x]

    a2 = each(stack, at)
    v2 = each(lambda sl: stack(v_ref[:, sl]), sls)
    pt = each(lambda z_, a_, t_: mm(z_.T, a_) + jnp.where(eye, jnp.exp(t_), 0.0), z, a2, tot)
    dd = each(lambda k_, z_, w_: mm(k_, z_) + stack(w_), mak, z, kw)
    qt = each(lambda d_, v_: mm(d_.T, v_), dd, v2)
    g = each(lambda r_, t_, a_: stack(r_) + mm(t_.T, a_), rt, tm_, a2)
    hm = each(lambda kr_, k_, t_: kr_ + mm(k_, t_), mkr, mak, tm_)
    hv = each(lambda h_, v_: mm(h_.T, v_), hm, v2)
    st = [st_ref[p] for p in pairs]
    ys = each(lambda g_, s_, h_: mm(g_, s_) + h_, g, st, hv)
    new = each(lambda p_, s_, q_: mm(p_, s_) + q_, pt, st, qt)
    for p in pairs:
        y_ref[:, sls[p]] = ys[p][:ch] + ys[p][ch:]
        st_ref[p] = new[p]


def _rwkv_scan(r, v, kk, lw, kd, bb, n_ctx):
    b, t, w = r.shape
    nc, ncx = t // CHUNK, n_ctx // CHUNK

    def chunk(d, c):
        return jnp.where(d == 0, c, jnp.where(c < ncx, ncx - 1 - c, nc - 1 + ncx - c))

    shared = pl.BlockSpec((None, CHUNK, w), lambda bb_, d, c: (bb_, chunk(d, c), 0))
    perdir = pl.BlockSpec((None, None, CHUNK, w), lambda bb_, d, c: (bb_, d, chunk(d, c), 0))
    return pl.pallas_call(
        _scan_kernel,
        grid=(b, N_DIRS, nc),
        in_specs=[shared] * 3 + [perdir] * 3,
        out_specs=perdir,
        out_shape=jax.ShapeDtypeStruct((b, N_DIRS, t, w), F32),
        scratch_shapes=[pltpu.VMEM((w // LANES, LANES, LANES), F32)],
        compiler_params=_params(("parallel", "parallel", "arbitrary")),
        name="rwkv_scan",
    )(r, v, kk, lw, kd, bb)


def _out_kernel(y0_ref, y1_ref, bonus_ref, gate_ref, attn_ref, x_ref, lnw_ref, lnb_ref, w_ref, g_ref, o_ref):
    ones = _head_ones(256)
    y = y0_ref[...] + y1_ref[...]
    yc = y - _head_sum(y, ones) * (1.0 / HEAD_DIM)
    var = _head_sum(yc * yc, ones) * (1.0 / HEAD_DIM)
    yn = yc * lax.rsqrt(var + GN_EPS) * lnw_ref[...] + lnb_ref[...]
    rw = ((yn + bonus_ref[...]) * gate_ref[...]).astype(BF16)
    mix = jnp.dot(jnp.concatenate([attn_ref[...], rw], axis=1), w_ref[...], preferred_element_type=F32)
    o_ref[...] = x_ref[...] + g_ref[...] * mix


def _mixer_out(y, bonus, gate, attn, xs, ln_w, ln_b, w_out, mod):
    b, t, d = xs.shape
    tm = TOKEN_TILE
    tile = lambda w: pl.BlockSpec((None, tm, w), lambda bb, i: (bb, i, 0))
    ydir = lambda dd: pl.BlockSpec((None, None, tm, RWKV_DIM), lambda bb, i: (bb, dd, i, 0))
    row = lambda w: pl.BlockSpec((1, w), lambda bb, i: (0, 0))
    return pl.pallas_call(
        _out_kernel,
        grid=(b, t // tm),
        in_specs=[ydir(0), ydir(1), tile(RWKV_DIM), tile(RWKV_DIM), tile(ATTN_DIM), tile(d),
                  row(RWKV_DIM), row(RWKV_DIM), pl.BlockSpec(w_out.shape, lambda bb, i: (0, 0)),
                  _mod_spec(b, 2, d)],
        out_specs=tile(d),
        out_shape=jax.ShapeDtypeStruct((b, t, d), F32),
        compiler_params=_params(("parallel", "parallel")),
        name="mixer_out",
    )(y, y, bonus, gate, attn, xs, ln_w, ln_b, w_out, mod)


def _query_kernel(x_ref, gain_ref, sh_ref, sc_ref, w_ref, k1_ref, k2_ref, ht_ref, s1_ref, s2_ref):
    hf = _norm_mod(x_ref[...], gain_ref[...], sh_ref[...], sc_ref[...])
    ht_ref[...] = hf.T.astype(BF16)
    q = jnp.dot(hf.astype(BF16), w_ref[...], preferred_element_type=F32).astype(BF16)
    nt = (((1,), (1,)), ((), ()))
    for hd in range(PEER_HEADS):
        c0 = hd * 2 * PEER_HALF
        s1_ref[hd] = lax.dot_general(k1_ref[...], q[:, c0:c0 + PEER_HALF], nt, preferred_element_type=F32)
        s2_ref[hd] = lax.dot_general(k2_ref[...], q[:, c0 + PEER_HALF:c0 + 2 * PEER_HALF], nt,
                                     preferred_element_type=F32)


def _peer_scores(xs, mod, gain, w_query, keys1, keys2):
    b, t, d = xs.shape
    tm = TOKEN_TILE
    nt = t // tm
    row = lambda w: pl.BlockSpec((1, w), lambda bb, i: (0, 0))
    full = lambda a: pl.BlockSpec(a.shape, lambda bb, i: (0,) * a.ndim)
    sspec = pl.BlockSpec((PEER_HEADS, N_KEYS, tm), lambda bb, i: (0, 0, bb * nt + i))
    sshape = jax.ShapeDtypeStruct((PEER_HEADS, N_KEYS, b * t), F32)
    return pl.pallas_call(
        _query_kernel,
        grid=(b, nt),
        in_specs=[pl.BlockSpec((None, tm, d), lambda bb, i: (bb, i, 0)), row(d), _mod_spec(b, 3, d),
                  _mod_spec(b, 4, d), full(w_query), full(keys1), full(keys2)],
        out_specs=[pl.BlockSpec((d, tm), lambda bb, i: (0, bb * nt + i)), sspec, sspec],
        out_shape=[jax.ShapeDtypeStruct((d, b * t), BF16), sshape, sshape],
        compiler_params=_params(("parallel", "parallel")),
        name="peer_scores",
    )(xs, gain, mod, mod, w_query, keys1, keys2)


def _top_rows(x_ref, out_ref):
    slabs, n, tl = x_ref.shape
    idx = _iota((n, tl), 0).astype(F32)

    def body(i, carry):
        xs = [x_ref[s] for s in range(slabs)]
        ms = [jnp.max(x, axis=0, keepdims=True) for x in xs]
        firsts = [jnp.min(jnp.where(x == m, idx, float(n)), axis=0, keepdims=True) for x, m in zip(xs, ms)]
        for s in range(slabs):
            out_ref[s, pl.ds(i, 1), :] = ms[s]
            x_ref[s] = jnp.where(idx == firsts[s], -jnp.inf, xs[s])
        return carry

    lax.fori_loop(0, _RANKS, body, 0)


_RANKS = PEER_TOPK + 1
_RANK_ROWS = -(-_RANKS // 8) * 8
_CAND_COUNTS = [_RANKS // (i + 1) for i in range(_RANKS)]
_CAND_ROWS = -(-sum(_CAND_COUNTS) // 8) * 8


def _topk_kernel(s1_ref, s2_ref, thr_ref, lse_ref, work_ref, v_ref, cand_ref, top_ref):
    work_ref[:PEER_HEADS] = s1_ref[...]
    work_ref[PEER_HEADS:] = s2_ref[...]
    _top_rows(work_ref, v_ref)
    cand_ref[...] = jnp.full(cand_ref.shape, -jnp.inf, F32)
    for hd in range(PEER_HEADS):
        off = 0
        for i, cnt in enumerate(_CAND_COUNTS):
            cand_ref[hd, off:off + cnt, :] = v_ref[hd, i:i + 1, :] + v_ref[PEER_HEADS + hd, 0:cnt, :]
            off += cnt
    _top_rows(cand_ref, top_ref)
    for hd in range(PEER_HEADS):
        top = top_ref[hd, 0:PEER_TOPK, :]
        mx = top[0:1, :]
        thr_ref[hd:hd + 1, :] = 0.5 * (top[PEER_TOPK - 1:PEER_TOPK, :] + top_ref[hd, PEER_TOPK:_RANKS, :])
        lse_ref[hd:hd + 1, :] = mx + jnp.log(jnp.sum(jnp.exp(top - mx), axis=0, keepdims=True))


def _peer_topk(s1, s2):
    _, _, n = s1.shape
    tl = TOKEN_TILE
    sspec = pl.BlockSpec((PEER_HEADS, N_KEYS, tl), lambda i: (0, 0, i))
    ospec = pl.BlockSpec((PEER_HEADS, tl), lambda i: (0, i))
    oshape = jax.ShapeDtypeStruct((PEER_HEADS, n), F32)
    return pl.pallas_call(
        _topk_kernel,
        grid=(n // tl,),
        in_specs=[sspec, sspec],
        out_specs=[ospec, ospec],
        out_shape=[oshape, oshape],
        scratch_shapes=[pltpu.VMEM((2 * PEER_HEADS, N_KEYS, tl), F32),
                        pltpu.VMEM((2 * PEER_HEADS, _RANK_ROWS, tl), F32),
                        pltpu.VMEM((PEER_HEADS, _CAND_ROWS, tl), F32),
                        pltpu.VMEM((PEER_HEADS, _RANK_ROWS, tl), F32)],
        compiler_params=_params(("parallel",)),
        name="peer_topk",
    )(s1, s2)


def _peer_kernel(ht_ref, s1_ref, s2_ref, thr_ref, lse_ref, u_ref, vt_ref, x_ref, g_ref, o_ref,
                 acc_ref, need_ref, e2_ref, off_ref, *, n_steps):
    ap = pl.program_id(2)

    @pl.when(ap == 0)
    def _():
        acc_ref[...] = jnp.zeros(acc_ref.shape, F32)
        for hd in range(PEER_HEADS):
            s2 = s2_ref[hd]
            m2 = jnp.max(s2, axis=0, keepdims=True)
            need_ref[hd] = thr_ref[hd:hd + 1, :] - s2
            e2_ref[hd] = jnp.exp(s2 - m2)
            off_ref[hd:hd + 1, :] = m2 - lse_ref[hd:hd + 1, :]

    ht = ht_ref[...]
    sub, rows = EXPERT_SUB, EXPERT_SUB // N_KEYS

    def pre_act(j):
        return jnp.dot(u_ref[j * sub:(j + 1) * sub, :], ht, preferred_element_type=F32)

    pre, upd, part = pre_act(0), None, None
    for j in range(EXPERT_TILE // sub):
        nxt = pre_act(j + 1) if (j + 1) * sub < EXPERT_TILE else None
        gates = []
        for r in range(rows):
            a = ap * (EXPERT_TILE // N_KEYS) + j * rows + r
            w = None
            for hd in range(PEER_HEADS):
                s1a = s1_ref[hd, pl.ds(a, 1), :]
                e = jnp.where(need_ref[hd] <= s1a, e2_ref[hd] * jnp.exp(s1a + off_ref[hd:hd + 1, :]), 0.0)
                w = e if w is None else w + e
            gates.append(w)
        if part is not None:
            upd = part if upd is None else upd + part
        act = 0.5 * pre * (1.0 + lax.erf(pre * (2.0 ** -0.5)))
        gw = (jnp.concatenate(gates, axis=0) * act).astype(BF16)
        part = jnp.dot(vt_ref[:, j * sub:(j + 1) * sub], gw, preferred_element_type=F32)
        pre = nxt
    acc_ref[...] += upd + part

    @pl.when(ap == n_steps - 1)
    def _():
        o_ref[...] = x_ref[...] + g_ref[...] * acc_ref[...].T


def _peer_experts(ht, s1, s2, thr, lse, u, vt, xs, mod):
    b, t, d = xs.shape
    tm = TOKEN_TILE
    nt = t // tm
    n_steps = u.shape[0] // EXPERT_TILE
    tok = lambda bb, i, a: bb * nt + i
    sspec = pl.BlockSpec((PEER_HEADS, N_KEYS, tm), lambda bb, i, a: (0, 0, tok(bb, i, a)))
    rspec = pl.BlockSpec((PEER_HEADS, tm), lambda bb, i, a: (0, tok(bb, i, a)))
    tile = pl.BlockSpec((None, tm, d), lambda bb, i, a: (bb, i, 0))
    return pl.pallas_call(
        functools.partial(_peer_kernel, n_steps=n_steps),
        grid=(b, nt, n_steps),
        in_specs=[pl.BlockSpec((d, tm), lambda bb, i, a: (0, tok(bb, i, a))), sspec, sspec, rspec, rspec,
                  pl.BlockSpec((EXPERT_TILE, d), lambda bb, i, a: (a, 0)),
                  pl.BlockSpec((d, EXPERT_TILE), lambda bb, i, a: (0, a)),
                  tile, _mod_spec(b, 5, d)],
        out_specs=tile,
        out_shape=jax.ShapeDtypeStruct((b, t, d), F32),
        scratch_shapes=[pltpu.VMEM((d, tm), F32), pltpu.VMEM((PEER_HEADS, N_KEYS, tm), F32),
                        pltpu.VMEM((PEER_HEADS, N_KEYS, tm), F32), pltpu.VMEM((PEER_HEADS, tm), F32)],
        compiler_params=_params(("parallel", "parallel", "arbitrary")),
        name="peer_experts",
    )(ht, s1, s2, thr, lse, u, vt, xs, mod)


def _rope_tables(n_lat, n_ctx):
    pos = jnp.arange(n_lat)
    inv_freq = ROPE_THETA ** (-jnp.arange(AXIS_FREQS, dtype=F32) * 2.0 / AXIS_ROT_DIM)
    ang = jnp.stack([(pos // GRID_W).astype(F32)[:, None] * inv_freq,
                     (pos % GRID_W).astype(F32)[:, None] * inv_freq], axis=1)
    cos = jnp.broadcast_to(jnp.cos(ang)[:, :, None, :], (n_lat, 2, 2, AXIS_FREQS)).reshape(n_lat, HEAD_DIM)
    sin = jnp.stack([-jnp.sin(ang), jnp.sin(ang)], axis=2).reshape(n_lat, HEAD_DIM)
    cos = jnp.concatenate([jnp.ones((n_ctx, HEAD_DIM), F32), cos], axis=0)
    sin = jnp.concatenate([jnp.zeros((n_ctx, HEAD_DIM), F32), sin], axis=0)
    return jnp.tile(cos, (1, LANES // HEAD_DIM)), jnp.tile(sin, (1, LANES // HEAD_DIM))


def _pad_lora(up):
    z = jnp.zeros_like(up[0])
    return jnp.stack([jnp.concatenate([up[0], z], axis=0), jnp.concatenate([z, up[1]], axis=0)]).astype(BF16)


def kernel(x, c, ctx, c_ctx, mod_w, mod_b, norm_mix, norm_ffn, w_in, q_gain, k_gain, shift_taps, decay_base,
           decay_up, iclr_base, iclr_up, gate_up, k_k, k_a, r_k, ln_x_w, ln_x_b, w_out, peer_query, peer_subkeys1,
           peer_subkeys2, expert_u, expert_v):
    b, n_lat, d = x.shape
    n_ctx = ctx.shape[1]
    depth = mod_w.shape[0]
    assert n_ctx == TOKEN_TILE and n_lat % TOKEN_TILE == 0 and b < 8

    xs = jnp.concatenate([ctx, x], axis=1)
    cc = jnp.concatenate([c, c_ctx[None, :], jnp.zeros((8 - b - 1, d), F32)], axis=0)
    mod_all = _modulation(cc, mod_w, mod_b)
    cos_t, sin_t = _rope_tables(n_lat, n_ctx)
    row = lambda a: a.reshape(1, -1)

    for l in range(depth):
        mod = mod_all[l].reshape(8, 1, N_MOD * d)
        q, k, v, rw = _in_proj(xs, mod, row(norm_mix[l]), w_in[l].astype(BF16),
                               row(jnp.tile(q_gain[l], ATTN_HEADS)), row(jnp.tile(k_gain[l], ATTN_KV_HEADS)),
                               cos_t, sin_t)
        attn = _attention(q, k, v, n_ctx)
        r, vv, kk, gate, bonus, lw, kd, bb = _rwkv_features(
            rw, shift_taps[l], decay_base[l], _pad_lora(decay_up[l]), iclr_base[l], _pad_lora(iclr_up[l]),
            gate_up[l].astype(BF16), row(k_k[l]), row(k_a[l]), row(r_k[l]))
        y = _rwkv_scan(r, vv, kk, lw, kd, bb, n_ctx)
        xs = _mixer_out(y, bonus, gate, attn, xs, row(ln_x_w[l]), row(ln_x_b[l]), w_out[l].astype(BF16), mod)
        ht, s1, s2 = _peer_scores(xs, mod, row(norm_ffn[l]), peer_query[l].astype(BF16),
                                  peer_subkeys1[l].astype(BF16), peer_subkeys2[l].astype(BF16))
        thr, lse = _peer_topk(s1, s2)
        xs = _peer_experts(ht, s1, s2, thr, lse, expert_u[l].astype(BF16), expert_v[l].T.astype(BF16), xs, mod)
    return xs[:, n_ctx:, :]
```

```python
import functools
import math

import jax
import jax.numpy as jnp
from jax import lax
from jax.experimental import pallas as pl
from jax.experimental.pallas import tpu as pltpu

F32, BF16 = jnp.float32, jnp.bfloat16
HIGHEST = lax.Precision.HIGHEST

HEAD_DIM = 64
ATTN_HEADS = 8
ATTN_KV_HEADS = 2
ATTN_GROUP = ATTN_HEADS // ATTN_KV_HEADS
ATTN_DIM = ATTN_HEADS * HEAD_DIM
KV_DIM = ATTN_KV_HEADS * HEAD_DIM
ATTN_COLS = ATTN_DIM + 2 * KV_DIM
GRID_W = 64
ROPE_THETA = 10000.0
AXIS_ROT_DIM = HEAD_DIM // 2
AXIS_FREQS = AXIS_ROT_DIM // 2
RWKV_DIM = 512
LORA_COLS = 128
N_DIRS = 2
RWKV_COLS = 3 * RWKV_DIM + 3 * LORA_COLS
N_KEYS = 128
PEER_HEADS = 8
PEER_HALF = 128
PEER_TOPK = 16
N_MOD = 6
NORM_EPS = 1e-6
GN_EPS = 64e-5
L2_EPS = 1e-12

LANES = 128
TOKEN_TILE = 256
KV_TILE = 2816
KEY_SUB = 256
ONES_ROWS = 16
CHUNK = 64
SCAN_STEP = 256
EXPERT_TILE = 2048
EXPERT_SUB = 256
VMEM_LIMIT = 48 * 1024 * 1024


def _params(sem):
    return pltpu.CompilerParams(dimension_semantics=sem, vmem_limit_bytes=VMEM_LIMIT)


def _iota(shape, axis):
    return lax.broadcasted_iota(jnp.int32, shape, axis)


def _head_ones(n):
    return ((_iota((n, n), 0) >> 6) == (_iota((n, n), 1) >> 6)).astype(BF16)


def _head_sum(x, ones):
    n = ones.shape[0]
    outs = []
    for c0 in range(0, x.shape[1], n):
        rest = x[:, c0:c0 + n]
        acc = None
        for _ in range(3):
            part = rest.astype(BF16)
            rest = rest - part.astype(F32)
            t = jnp.dot(part, ones, preferred_element_type=F32)
            acc = t if acc is None else acc + t
        outs.append(acc)
    return outs[0] if len(outs) == 1 else jnp.concatenate(outs, axis=1)


def _norm_mod(x, gain, shift, scale):
    ms = jnp.mean(x * x, axis=-1, keepdims=True)
    return (x * lax.rsqrt(ms + NORM_EPS) * gain) * (1.0 + scale) + shift


def _mod_kernel(c_ref, w_ref, b_ref, o_ref):
    c = c_ref[...]
    a = c * jax.nn.sigmoid(c)
    o_ref[...] = jnp.dot(a, w_ref[...], preferred_element_type=F32, precision=HIGHEST) + b_ref[...]


def _modulation(cc, mod_w, mod_b):
    depth, d, n = mod_w.shape
    return pl.pallas_call(
        _mod_kernel,
        grid=(depth, n // d),
        in_specs=[pl.BlockSpec((8, d), lambda l, j: (0, 0)),
                  pl.BlockSpec((None, d, d), lambda l, j: (l, 0, j)),
                  pl.BlockSpec((None, 1, d), lambda l, j: (l, 0, j))],
        out_specs=pl.BlockSpec((None, 8, d), lambda l, j: (l, 0, j)),
        out_shape=jax.ShapeDtypeStruct((depth, 8, n), F32),
        compiler_params=_params(("parallel", "parallel")),
        name="modulation",
    )(cc, mod_w, mod_b.reshape(depth, 1, n))


def _mod_spec(batch, chunk, d):
    return pl.BlockSpec((None, 1, d), lambda b, i, *_: (jnp.where(i == 0, batch, b), 0, chunk))


def _rope(x, cos, sin):
    w = x.shape[1]
    reps = w // LANES
    if reps > 1:
        cos = jnp.concatenate([cos] * reps, axis=1)
        sin = jnp.concatenate([sin] * reps, axis=1)
    first = (_iota(x.shape, 1) & AXIS_FREQS) == 0
    partner = jnp.where(first, pltpu.roll(x, w - AXIS_FREQS, 1), pltpu.roll(x, AXIS_FREQS, 1))
    return x * cos + partner * sin


def _head_norm(x, gain, ones):
    ms = _head_sum(x * x, ones) * (1.0 / HEAD_DIM)
    return x * lax.rsqrt(ms + NORM_EPS) * gain


def _in_kernel(x_ref, gain_ref, sh_ref, sc_ref, w_ref, qg_ref, kg_ref, cos_ref, sin_ref,
               q_ref, k_ref, vt_ref, rw_ref):
    h = _norm_mod(x_ref[...], gain_ref[...], sh_ref[...], sc_ref[...])
    p = jnp.dot(h.astype(BF16), w_ref[...], preferred_element_type=F32)
    cos, sin = cos_ref[...], sin_ref[...]
    q = _head_norm(p[:, :ATTN_DIM], qg_ref[...], _head_ones(256))
    q_ref[...] = (_rope(q, cos, sin) * (HEAD_DIM ** -0.5)).astype(BF16)
    k = _head_norm(p[:, ATTN_DIM:ATTN_DIM + KV_DIM], kg_ref[...], _head_ones(KV_DIM))
    k_ref[...] = _rope(k, cos, sin).astype(BF16)
    vt_ref[...] = p[:, ATTN_DIM + KV_DIM:ATTN_COLS].T.astype(BF16)
    rw_ref[...] = p[:, ATTN_COLS:]


def _in_proj(xs, mod, gain, w_in, q_gain, k_gain, cos_t, sin_t):
    b, t, d = xs.shape
    tm = TOKEN_TILE
    n = w_in.shape[1]
    tile = lambda w: pl.BlockSpec((None, tm, w), lambda bb, i: (bb, i, 0))
    row = lambda w: pl.BlockSpec((1, w), lambda bb, i: (0, 0))
    return pl.pallas_call(
        _in_kernel,
        grid=(b, t // tm),
        in_specs=[tile(d), row(d), _mod_spec(b, 0, d), _mod_spec(b, 1, d),
                  pl.BlockSpec((d, n), lambda bb, i: (0, 0)), row(ATTN_DIM), row(KV_DIM),
                  pl.BlockSpec((tm, LANES), lambda bb, i: (i, 0)),
                  pl.BlockSpec((tm, LANES), lambda bb, i: (i, 0))],
        out_specs=[tile(ATTN_DIM), tile(KV_DIM), pl.BlockSpec((None, KV_DIM, tm), lambda bb, i: (bb, 0, i)),
                   tile(RWKV_COLS)],
        out_shape=[jax.ShapeDtypeStruct((b, t, ATTN_DIM), BF16),
                   jax.ShapeDtypeStruct((b, t, KV_DIM), BF16),
                   jax.ShapeDtypeStruct((b, KV_DIM, t), BF16),
                   jax.ShapeDtypeStruct((b, t, RWKV_COLS), F32)],
        compiler_params=_params(("parallel", "parallel")),
        name="in_proj",
    )(xs, gain, mod, mod, w_in, q_gain, k_gain, cos_t, sin_t)


def _attn_step(qt_ref, k_ref, vt_ref, m_ref, acc_ref, n_keys):
    groups = range(ATTN_KV_HEADS)
    ones = jnp.ones((ONES_ROWS, KEY_SUB), BF16)
    m = [m_ref[g] for g in groups]
    acc = [acc_ref[g] for g in groups]

    def scores(c):
        kc = k_ref[c * KEY_SUB:(c + 1) * KEY_SUB, :]
        return [jnp.dot(kc, qt_ref[g], preferred_element_type=F32) for g in groups]

    s = scores(0)
    for c in range(n_keys // KEY_SUB):
        nxt = scores(c + 1) if (c + 1) * KEY_SUB < n_keys else None
        for g in groups:
            m_new = jnp.maximum(m[g], jnp.max(s[g], axis=0, keepdims=True))
            alpha = jnp.exp(m[g] - m_new)
            p = jnp.exp(s[g] - m_new).astype(BF16)
            va = jnp.concatenate(
                [vt_ref[g * HEAD_DIM:(g + 1) * HEAD_DIM, c * KEY_SUB:(c + 1) * KEY_SUB], ones], axis=0)
            acc[g] = alpha * acc[g] + jnp.dot(va, p, preferred_element_type=F32)
            m[g] = m_new
        s = nxt
    for g in groups:
        m_ref[g] = m[g]
        acc_ref[g] = acc[g]


def _attn_kernel(q_ref, k_ref, vt_ref, o_ref, qt_ref, m_ref, acc_ref, *, n_ctx, nk):
    qi, kj = pl.program_id(1), pl.program_id(2)
    gw = ATTN_GROUP * HEAD_DIM

    @pl.when(kj == 0)
    def _():
        qt_ref[...] = jnp.zeros(qt_ref.shape, BF16)
        for g in range(ATTN_KV_HEADS):
            qg_t = q_ref[:, g * gw:(g + 1) * gw].astype(F32).T.astype(BF16)
            for h in range(ATTN_GROUP):
                qt_ref[g, g * HEAD_DIM:(g + 1) * HEAD_DIM, h * qg_t.shape[1]:(h + 1) * qg_t.shape[1]] = (
                    qg_t[h * HEAD_DIM:(h + 1) * HEAD_DIM, :])
        m_ref[...] = jnp.full(m_ref.shape, -jnp.inf, F32)
        acc_ref[...] = jnp.zeros(acc_ref.shape, F32)

    @pl.when((qi == 0) & (kj == 0))
    def _():
        _attn_step(qt_ref, k_ref, vt_ref, m_ref, acc_ref, n_ctx)

    @pl.when(qi > 0)
    def _():
        _attn_step(qt_ref, k_ref, vt_ref, m_ref, acc_ref, k_ref.shape[0])

    @pl.when(kj == nk - 1)
    def _():
        tq = q_ref.shape[0]
        for g in range(ATTN_KV_HEADS):
            o = acc_ref[g, :HEAD_DIM, :] / acc_ref[g, HEAD_DIM:HEAD_DIM + 1, :]
            o = jnp.concatenate([o[:, h * tq:(h + 1) * tq] for h in range(ATTN_GROUP)], axis=0)
            o_ref[:, g * gw:(g + 1) * gw] = o.T.astype(BF16)


def _attention(q, k, vt, n_ctx):
    b, t, _ = q.shape
    tq = TOKEN_TILE
    tk = KV_TILE if t % KV_TILE == 0 else t
    nk = t // tk
    assert n_ctx == tq and n_ctx % KEY_SUB == 0 and tk % KEY_SUB == 0
    kblock = lambda i, j: jnp.where(i == 0, 0, j)
    return pl.pallas_call(
        functools.partial(_attn_kernel, n_ctx=n_ctx, nk=nk),
        grid=(b, t // tq, nk),
        in_specs=[pl.BlockSpec((None, tq, ATTN_DIM), lambda bb, i, j: (bb, i, 0)),
                  pl.BlockSpec((None, tk, KV_DIM), lambda bb, i, j: (bb, kblock(i, j), 0)),
                  pl.BlockSpec((None, KV_DIM, tk), lambda bb, i, j: (bb, 0, kblock(i, j)))],
        out_specs=pl.BlockSpec((None, tq, ATTN_DIM), lambda bb, i, j: (bb, i, 0)),
        out_shape=jax.ShapeDtypeStruct((b, t, ATTN_DIM), BF16),
        scratch_shapes=[pltpu.VMEM((ATTN_KV_HEADS, KV_DIM, ATTN_GROUP * tq), BF16),
                        pltpu.VMEM((ATTN_KV_HEADS, 1, ATTN_GROUP * tq), F32),
                        pltpu.VMEM((ATTN_KV_HEADS, HEAD_DIM + ONES_ROWS, ATTN_GROUP * tq), F32)],
        compiler_params=_params(("parallel", "parallel", "arbitrary")),
        name="attention",
    )(q, k, vt)


def _feat_kernel(rw_ref, prev_ref, next_ref, taps_ref, dbase_ref, dup_ref, ibase_ref, iup_ref, gup_ref,
                 kk_par_ref, ka_ref, rk_ref,
                 r_out, v_out, kk_out, gate_out, bonus_out, lw_out, kd_out, bb_out, *, nt):
    i = pl.program_id(1)
    x = rw_ref[...]
    tm = x.shape[0]
    prev_row = jnp.where(i <= 1, 0.0, prev_ref[7:8, :])
    next_row = jnp.where((i == 0) | (i == nt - 1), 0.0, next_ref[0:1, :])
    row = _iota(x.shape, 0)
    xm = jnp.where(row == 0, prev_row, pltpu.roll(x, 1, 0))
    xp = jnp.where(row == tm - 1, next_row, pltpu.roll(x, tm - 1, 0))
    xs = xm * taps_ref[0:1, :] + x * taps_ref[1:2, :] + xp * taps_ref[2:3, :]

    o3 = 3 * RWKV_DIM
    r, k, v = xs[:, :RWKV_DIM], xs[:, RWKV_DIM:2 * RWKV_DIM], xs[:, 2 * RWKV_DIM:o3]
    wd = jnp.tanh(xs[:, o3:o3 + LORA_COLS]).astype(BF16)
    ad = xs[:, o3 + LORA_COLS:o3 + 2 * LORA_COLS].astype(BF16)
    gd = jax.nn.sigmoid(xs[:, o3 + 2 * LORA_COLS:]).astype(BF16)
    ones = _head_ones(256)
    kkr = k * kk_par_ref[...]
    kk = kkr * lax.rsqrt(_head_sum(kkr * kkr, ones) + L2_EPS)
    r_out[...] = r
    v_out[...] = v
    kk_out[...] = kk
    gate_out[...] = jnp.dot(gd, gup_ref[...], preferred_element_type=F32)
    bonus_out[...] = _head_sum(r * k * rk_ref[...], ones) * v
    for d in range(N_DIRS):
        z = dbase_ref[d:d + 1, :] + jnp.dot(wd, dup_ref[d], preferred_element_type=F32)
        lw_out[d] = jax.nn.sigmoid(z) * (-math.exp(-0.5))
        iclr = jax.nn.sigmoid(ibase_ref[d:d + 1, :] + jnp.dot(ad, iup_ref[d], preferred_element_type=F32))
        kd_out[d] = k * (1.0 + (iclr - 1.0) * ka_ref[...])
        bb_out[d] = kk * iclr


def _rwkv_features(rw, taps, dbase, dup, ibase, iup, gup, k_k, k_a, r_k):
    b, t, w = rw.shape
    tm = TOKEN_TILE
    nt = t // tm
    r8 = tm // 8
    full = lambda a: pl.BlockSpec(a.shape, lambda bb, i: (0,) * a.ndim)
    tile = pl.BlockSpec((None, tm, RWKV_DIM), lambda bb, i: (bb, i, 0))
    dtile = pl.BlockSpec((None, N_DIRS, tm, RWKV_DIM), lambda bb, i: (bb, 0, i, 0))
    shared = jax.ShapeDtypeStruct((b, t, RWKV_DIM), F32)
    perdir = jax.ShapeDtypeStruct((b, N_DIRS, t, RWKV_DIM), F32)
    consts = (taps, dbase, dup, ibase, iup, gup, k_k, k_a, r_k)
    return pl.pallas_call(
        functools.partial(_feat_kernel, nt=nt),
        grid=(b, nt),
        in_specs=[pl.BlockSpec((None, tm, w), lambda bb, i: (bb, i, 0)),
                  pl.BlockSpec((None, 8, w), lambda bb, i: (bb, jnp.maximum(i * r8 - 1, 0), 0)),
                  pl.BlockSpec((None, 8, w), lambda bb, i: (bb, jnp.minimum((i + 1) * r8, nt * r8 - 1), 0))]
                 + [full(a) for a in consts],
        out_specs=[tile] * 5 + [dtile] * 3,
        out_shape=[shared] * 5 + [perdir] * 3,
        compiler_params=_params(("parallel", "parallel")),
        name="rwkv_features",
    )(rw, rw, rw, *consts)


def _scan_kernel(r_ref, v_ref, kk_ref, lw_ref, kd_ref, bb_ref, y_ref, st_ref):
    rev = pl.program_id(1) == 1
    c = pl.program_id(2)

    @pl.when(c == 0)
    def _():
        st_ref[...] = jnp.zeros(st_ref.shape, F32)

    ch, pw = CHUNK, LANES
    ti, si = _iota((ch, ch), 0), _iota((ch, ch), 1)
    tri = (jnp.where(rev, si - ti, ti - si) >= 0).astype(F32)
    a, bcol = _iota((pw, pw), 0), _iota((pw, pw), 1)
    same = (a >> 6) == (bcol >> 6)
    ahead = jnp.where(rev, a - bcol, bcol - a)
    strict = same & (ahead > 0)
    incl = same & (ahead >= 0)
    eye = a == bcol
    head0 = _iota((ch, pw), 1) < HEAD_DIM

    def stack(x):
        return jnp.concatenate([jnp.where(head0, x, 0.0), jnp.where(head0, 0.0, x)], axis=0)

    def dup(x):
        return jnp.concatenate([x, x], axis=0)

    def mm(x, y):
        return jnp.dot(x.astype(BF16), y.astype(BF16), preferred_element_type=F32)

    pairs = range(RWKV_DIM // pw)
    n_sub = lw_ref.shape[0] // ch
    rows = [pl.ds(pl.multiple_of(jnp.where(rev, n_sub - 1 - j, j) * ch, ch), ch) for j in range(n_sub)]
    sls = [(rows[j], slice(p * pw, (p + 1) * pw)) for j in range(n_sub) for p in pairs]
    each = lambda f, *cols: [f(*args) for args in zip(*cols)]
    lw = [lw_ref[sl] for sl in sls]
    cum = each(lambda a_: jnp.dot(tri, a_, preferred_element_type=F32, precision=HIGHEST), lw)
    tot = each(lambda a_: jnp.sum(a_, axis=0, keepdims=True), lw)
    w_inv = each(lambda c_: jnp.exp(-c_), cum)
    w_rem = each(lambda t_, c_: jnp.exp(t_ - c_), tot, cum)
    at = each(lambda sl, c_, l_: -kk_ref[sl] * jnp.exp(c_ - l_), sls, cum, lw)
    rt = each(lambda sl, c_: r_ref[sl] * jnp.exp(c_), sls, cum)
    bt = each(lambda sl, w_: bb_ref[sl] * w_, sls, w_inv)
    kt = each(lambda sl, w_: kd_ref[sl] * w_, sls, w_inv)
    bw = each(lambda sl, w_: bb_ref[sl] * w_, sls, w_rem)
    kw = each(lambda sl, w_: kd_ref[sl] * w_, sls, w_rem)

    nt_dims = (((1,), (1,)), ((), ()))
    m = each(lambda b_, k_, a_, r_: lax.dot_general(
        jnp.concatenate([stack(b_), stack(k_)], axis=0).astype(BF16),
        jnp.concatenate([dup(a_), dup(r_)], axis=0).astype(BF16), nt_dims, preferred_element_type=F32),
        bt, kt, at, rt)
    mab = each(lambda m_: jnp.where(strict, m_[:pw, :pw], 0.0), m)
    mbr = each(lambda m_: jnp.where(incl, m_[:pw, pw:], 0.0), m)
    mak = each(lambda m_: jnp.where(strict, m_[pw:, :pw], 0.0), m)
    mkr = each(lambda m_: jnp.where(incl, m_[pw:, pw:], 0.0), m)

    n = mab
    x = each(lambda b_, m_: jnp.concatenate([stack(b_), m_], axis=1), bw, mbr)
    levels = CHUNK.bit_length() - 1
    for lvl in range(levels - 1):
        nx = each(lambda n_, x_: mm(n_, jnp.concatenate([n_, x_], axis=1)), n, x)
        n = [nx_[:, :pw] for nx_ in nx]
        x = each(lambda x_, nx_: x_ + nx_[:, pw:], x, nx)
    x = each(lambda n_, x_: x_ + mm(n_, x_), n, x)
    z = [x_[:, :pw] for x_ in x]
    tm_ = [x_[:, pw:] for x_ in x]

    a2 = each(stack, at)
    v2 = each(lambda sl: stack(v_ref[sl]), sls)
    pt = each(lambda z_, a_, t_: mm(z_.T, a_) + jnp.where(eye, jnp.exp(t_), 0.0), z, a2, tot)
    dd = each(lambda k_, z_, w_: mm(k_, z_) + stack(w_), mak, z, kw)
    qt = each(lambda d_, v_: mm(d_.T, v_), dd, v2)
    g = each(lambda r_, t_, a_: stack(r_) + mm(t_.T, a_), rt, tm_, a2)
    hm = each(lambda kr_, k_, t_: kr_ + mm(k_, t_), mkr, mak, tm_)
    hv = each(lambda h_, v_: mm(h_.T, v_), hm, v2)
    st = [st_ref[p] for p in pairs]
    for j in range(n_sub):
        part = slice(j * len(pairs), (j + 1) * len(pairs))
        ys = each(lambda g_, s_, h_: mm(g_, s_) + h_, g[part], st, hv[part])
        st = each(lambda p_, s_, q_: mm(p_, s_) + q_, pt[part], st, qt[part])
        for p in pairs:
            y_ref[sls[part][p]] = ys[p][:ch] + ys[p][ch:]
    for p in pairs:
        st_ref[p] = st[p]


def _rwkv_scan(r, v, kk, lw, kd, bb, n_ctx):
    b, t, w = r.shape
    assert t % SCAN_STEP == 0 and n_ctx % SCAN_STEP == 0 and SCAN_STEP % CHUNK == 0
    nc, ncx = t // SCAN_STEP, n_ctx // SCAN_STEP

    def chunk(d, c):
        return jnp.where(d == 0, c, jnp.where(c < ncx, ncx - 1 - c, nc - 1 + ncx - c))

    shared = pl.BlockSpec((None, SCAN_STEP, w), lambda bb_, d, c: (bb_, chunk(d, c), 0))
    perdir = pl.BlockSpec((None, None, SCAN_STEP, w), lambda bb_, d, c: (bb_, d, chunk(d, c), 0))
    return pl.pallas_call(
        _scan_kernel,
        grid=(b, N_DIRS, nc),
        in_specs=[shared] * 3 + [perdir] * 3,
        out_specs=perdir,
        out_shape=jax.ShapeDtypeStruct((b, N_DIRS, t, w), F32),
        scratch_shapes=[pltpu.VMEM((w // LANES, LANES, LANES), F32)],
        compiler_params=_params(("parallel", "parallel", "arbitrary")),
        name="rwkv_scan",
    )(r, v, kk, lw, kd, bb)


def _out_kernel(y0_ref, y1_ref, bonus_ref, gate_ref, attn_ref, x_ref, lnw_ref, lnb_ref, w_ref, g_ref, o_ref):
    ones = _head_ones(256)
    y = y0_ref[...] + y1_ref[...]
    yc = y - _head_sum(y, ones) * (1.0 / HEAD_DIM)
    var = _head_sum(yc * yc, ones) * (1.0 / HEAD_DIM)
    yn = yc * lax.rsqrt(var + GN_EPS) * lnw_ref[...] + lnb_ref[...]
    rw = ((yn + bonus_ref[...]) * gate_ref[...]).astype(BF16)
    mix = jnp.dot(jnp.concatenate([attn_ref[...], rw], axis=1), w_ref[...], preferred_element_type=F32)
    o_ref[...] = x_ref[...] + g_ref[...] * mix


def _mixer_out(y, bonus, gate, attn, xs, ln_w, ln_b, w_out, mod):
    b, t, d = xs.shape
    tm = TOKEN_TILE
    tile = lambda w: pl.BlockSpec((None, tm, w), lambda bb, i: (bb, i, 0))
    ydir = lambda dd: pl.BlockSpec((None, None, tm, RWKV_DIM), lambda bb, i: (bb, dd, i, 0))
    row = lambda w: pl.BlockSpec((1, w), lambda bb, i: (0, 0))
    return pl.pallas_call(
        _out_kernel,
        grid=(b, t // tm),
        in_specs=[ydir(0), ydir(1), tile(RWKV_DIM), tile(RWKV_DIM), tile(ATTN_DIM), tile(d),
                  row(RWKV_DIM), row(RWKV_DIM), pl.BlockSpec(w_out.shape, lambda bb, i: (0, 0)),
                  _mod_spec(b, 2, d)],
        out_specs=tile(d),
        out_shape=jax.ShapeDtypeStruct((b, t, d), F32),
        compiler_params=_params(("parallel", "parallel")),
        name="mixer_out",
    )(y, y, bonus, gate, attn, xs, ln_w, ln_b, w_out, mod)


def _query_kernel(x_ref, gain_ref, sh_ref, sc_ref, w_ref, k1_ref, k2_ref, ht_ref, s1_ref, s2_ref):
    hf = _norm_mod(x_ref[...], gain_ref[...], sh_ref[...], sc_ref[...])
    ht_ref[...] = hf.T.astype(BF16)
    q = jnp.dot(hf.astype(BF16), w_ref[...], preferred_element_type=F32).astype(BF16)
    nt = (((1,), (1,)), ((), ()))
    for hd in range(PEER_HEADS):
        c0 = hd * 2 * PEER_HALF
        s1_ref[hd] = lax.dot_general(k1_ref[...], q[:, c0:c0 + PEER_HALF], nt, preferred_element_type=F32)
        s2_ref[hd] = lax.dot_general(k2_ref[...], q[:, c0 + PEER_HALF:c0 + 2 * PEER_HALF], nt,
                                     preferred_element_type=F32)


def _peer_scores(xs, mod, gain, w_query, keys1, keys2):
    b, t, d = xs.shape
    tm = TOKEN_TILE
    nt = t // tm
    row = lambda w: pl.BlockSpec((1, w), lambda bb, i: (0, 0))
    full = lambda a: pl.BlockSpec(a.shape, lambda bb, i: (0,) * a.ndim)
    sspec = pl.BlockSpec((PEER_HEADS, N_KEYS, tm), lambda bb, i: (0, 0, bb * nt + i))
    sshape = jax.ShapeDtypeStruct((PEER_HEADS, N_KEYS, b * t), F32)
    return pl.pallas_call(
        _query_kernel,
        grid=(b, nt),
        in_specs=[pl.BlockSpec((None, tm, d), lambda bb, i: (bb, i, 0)), row(d), _mod_spec(b, 3, d),
                  _mod_spec(b, 4, d), full(w_query), full(keys1), full(keys2)],
        out_specs=[pl.BlockSpec((d, tm), lambda bb, i: (0, bb * nt + i)), sspec, sspec],
        out_shape=[jax.ShapeDtypeStruct((d, b * t), BF16), sshape, sshape],
        compiler_params=_params(("parallel", "parallel")),
        name="peer_scores",
    )(xs, gain, mod, mod, w_query, keys1, keys2)


def _top_rows(x_ref, out_ref):
    slabs, n, tl = x_ref.shape
    idx = _iota((n, tl), 0).astype(F32)

    def body(i, carry):
        xs = [x_ref[s] for s in range(slabs)]
        ms = [jnp.max(x, axis=0, keepdims=True) for x in xs]
        firsts = [jnp.min(jnp.where(x == m, idx, float(n)), axis=0, keepdims=True) for x, m in zip(xs, ms)]
        for s in range(slabs):
            out_ref[s, pl.ds(i, 1), :] = ms[s]
            x_ref[s] = jnp.where(idx == firsts[s], -jnp.inf, xs[s])
        return carry

    lax.fori_loop(0, _RANKS, body, 0)


_RANKS = PEER_TOPK + 1
_RANK_ROWS = -(-_RANKS // 8) * 8
_CAND_COUNTS = [_RANKS // (i + 1) for i in range(_RANKS)]
_CAND_ROWS = -(-sum(_CAND_COUNTS) // 8) * 8


def _topk_kernel(s1_ref, s2_ref, thr_ref, lse_ref, work_ref, v_ref, cand_ref, top_ref):
    work_ref[:PEER_HEADS] = s1_ref[...]
    work_ref[PEER_HEADS:] = s2_ref[...]
    _top_rows(work_ref, v_ref)
    cand_ref[...] = jnp.full(cand_ref.shape, -jnp.inf, F32)
    for hd in range(PEER_HEADS):
        off = 0
        for i, cnt in enumerate(_CAND_COUNTS):
            cand_ref[hd, off:off + cnt, :] = v_ref[hd, i:i + 1, :] + v_ref[PEER_HEADS + hd, 0:cnt, :]
            off += cnt
    _top_rows(cand_ref, top_ref)
    for hd in range(PEER_HEADS):
        top = top_ref[hd, 0:PEER_TOPK, :]
        mx = top[0:1, :]
        thr_ref[hd:hd + 1, :] = 0.5 * (top[PEER_TOPK - 1:PEER_TOPK, :] + top_ref[hd, PEER_TOPK:_RANKS, :])
        lse_ref[hd:hd + 1, :] = mx + jnp.log(jnp.sum(jnp.exp(top - mx), axis=0, keepdims=True))


def _peer_topk(s1, s2):
    _, _, n = s1.shape
    tl = TOKEN_TILE
    sspec = pl.BlockSpec((PEER_HEADS, N_KEYS, tl), lambda i: (0, 0, i))
    ospec = pl.BlockSpec((PEER_HEADS, tl), lambda i: (0, i))
    oshape = jax.ShapeDtypeStruct((PEER_HEADS, n), F32)
    return pl.pallas_call(
        _topk_kernel,
        grid=(n // tl,),
        in_specs=[sspec, sspec],
        out_specs=[ospec, ospec],
        out_shape=[oshape, oshape],
        scratch_shapes=[pltpu.VMEM((2 * PEER_HEADS, N_KEYS, tl), F32),
                        pltpu.VMEM((2 * PEER_HEADS, _RANK_ROWS, tl), F32),
                        pltpu.VMEM((PEER_HEADS, _CAND_ROWS, tl), F32),
                        pltpu.VMEM((PEER_HEADS, _RANK_ROWS, tl), F32)],
        compiler_params=_params(("parallel",)),
        name="peer_topk",
    )(s1, s2)


def _peer_kernel(ht_ref, s1_ref, s2_ref, thr_ref, lse_ref, u_ref, vt_ref, x_ref, g_ref, o_ref,
                 acc_ref, need_ref, e2_ref, off_ref, *, n_steps):
    ap = pl.program_id(2)

    @pl.when(ap == 0)
    def _():
        acc_ref[...] = jnp.zeros(acc_ref.shape, F32)
        for hd in range(PEER_HEADS):
            s2 = s2_ref[hd]
            m2 = jnp.max(s2, axis=0, keepdims=True)
            need_ref[hd] = thr_ref[hd:hd + 1, :] - s2
            e2_ref[hd] = jnp.exp(s2 - m2)
            off_ref[hd:hd + 1, :] = m2 - lse_ref[hd:hd + 1, :]

    ht = ht_ref[...]
    sub, rows = EXPERT_SUB, EXPERT_SUB // N_KEYS

    def pre_act(j):
        return jnp.dot(u_ref[j * sub:(j + 1) * sub, :], ht, preferred_element_type=F32)

    pre, upd, part = pre_act(0), None, None
    for j in range(EXPERT_TILE // sub):
        nxt = pre_act(j + 1) if (j + 1) * sub < EXPERT_TILE else None
        gates = []
        for r in range(rows):
            a = ap * (EXPERT_TILE // N_KEYS) + j * rows + r
            w = None
            for hd in range(PEER_HEADS):
                s1a = s1_ref[hd, pl.ds(a, 1), :]
                e = jnp.where(need_ref[hd] <= s1a, e2_ref[hd] * jnp.exp(s1a + off_ref[hd:hd + 1, :]), 0.0)
                w = e if w is None else w + e
            gates.append(w)
        if part is not None:
            upd = part if upd is None else upd + part
        act = 0.5 * pre * (1.0 + lax.erf(pre * (2.0 ** -0.5)))
        gw = (jnp.concatenate(gates, axis=0) * act).astype(BF16)
        part = jnp.dot(vt_ref[:, j * sub:(j + 1) * sub], gw, preferred_element_type=F32)
        pre = nxt
    acc_ref[...] += upd + part

    @pl.when(ap == n_steps - 1)
    def _():
        o_ref[...] = x_ref[...] + g_ref[...] * acc_ref[...].T


def _peer_experts(ht, s1, s2, thr, lse, u, vt, xs, mod):
    b, t, d = xs.shape
    tm = TOKEN_TILE
    nt = t // tm
    n_steps = u.shape[0] // EXPERT_TILE
    tok = lambda bb, i, a: bb * nt + i
    sspec = pl.BlockSpec((PEER_HEADS, N_KEYS, tm), lambda bb, i, a: (0, 0, tok(bb, i, a)))
    rspec = pl.BlockSpec((PEER_HEADS, tm), lambda bb, i, a: (0, tok(bb, i, a)))
    tile = pl.BlockSpec((None, tm, d), lambda bb, i, a: (bb, i, 0))
    return pl.pallas_call(
        functools.partial(_peer_kernel, n_steps=n_steps),
        grid=(b, nt, n_steps),
        in_specs=[pl.BlockSpec((d, tm), lambda bb, i, a: (0, tok(bb, i, a))), sspec, sspec, rspec, rspec,
                  pl.BlockSpec((EXPERT_TILE, d), lambda bb, i, a: (a, 0)),
                  pl.BlockSpec((d, EXPERT_TILE), lambda bb, i, a: (0, a)),
                  tile, _mod_spec(b, 5, d)],
        out_specs=tile,
        out_shape=jax.ShapeDtypeStruct((b, t, d), F32),
        scratch_shapes=[pltpu.VMEM((d, tm), F32), pltpu.VMEM((PEER_HEADS, N_KEYS, tm), F32),
                        pltpu.VMEM((PEER_HEADS, N_KEYS, tm), F32), pltpu.VMEM((PEER_HEADS, tm), F32)],
        compiler_params=_params(("parallel", "parallel", "arbitrary")),
        name="peer_experts",
    )(ht, s1, s2, thr, lse, u, vt, xs, mod)


def _rope_tables(n_lat, n_ctx):
    pos = jnp.arange(n_lat)
    inv_freq = ROPE_THETA ** (-jnp.arange(AXIS_FREQS, dtype=F32) * 2.0 / AXIS_ROT_DIM)
    ang = jnp.stack([(pos // GRID_W).astype(F32)[:, None] * inv_freq,
                     (pos % GRID_W).astype(F32)[:, None] * inv_freq], axis=1)
    cos = jnp.broadcast_to(jnp.cos(ang)[:, :, None, :], (n_lat, 2, 2, AXIS_FREQS)).reshape(n_lat, HEAD_DIM)
    sin = jnp.stack([-jnp.sin(ang), jnp.sin(ang)], axis=2).reshape(n_lat, HEAD_DIM)
    cos = jnp.concatenate([jnp.ones((n_ctx, HEAD_DIM), F32), cos], axis=0)
    sin = jnp.concatenate([jnp.zeros((n_ctx, HEAD_DIM), F32), sin], axis=0)
    return jnp.tile(cos, (1, LANES // HEAD_DIM)), jnp.tile(sin, (1, LANES // HEAD_DIM))


def _pad_lora(up):
    z = jnp.zeros_like(up[0])
    return jnp.stack([jnp.concatenate([up[0], z], axis=0), jnp.concatenate([z, up[1]], axis=0)]).astype(BF16)


def kernel(x, c, ctx, c_ctx, mod_w, mod_b, norm_mix, norm_ffn, w_in, q_gain, k_gain, shift_taps, decay_base,
           decay_up, iclr_base, iclr_up, gate_up, k_k, k_a, r_k, ln_x_w, ln_x_b, w_out, peer_query, peer_subkeys1,
           peer_subkeys2, expert_u, expert_v):
    b, n_lat, d = x.shape
    n_ctx = ctx.shape[1]
    depth = mod_w.shape[0]
    assert n_ctx == TOKEN_TILE and n_lat % TOKEN_TILE == 0 and b < 8

    xs = jnp.concatenate([ctx, x], axis=1)
    cc = jnp.concatenate([c, c_ctx[None, :], jnp.zeros((8 - b - 1, d), F32)], axis=0)
    mod_all = _modulation(cc, mod_w, mod_b)
    cos_t, sin_t = _rope_tables(n_lat, n_ctx)
    row = lambda a: a.reshape(1, -1)

    for l in range(depth):
        mod = mod_all[l].reshape(8, 1, N_MOD * d)
        q, k, v, rw = _in_proj(xs, mod, row(norm_mix[l]), w_in[l].astype(BF16),
                               row(jnp.tile(q_gain[l], ATTN_HEADS)), row(jnp.tile(k_gain[l], ATTN_KV_HEADS)),
                               cos_t, sin_t)
        attn = _attention(q, k, v, n_ctx)
        r, vv, kk, gate, bonus, lw, kd, bb = _rwkv_features(
            rw, shift_taps[l], decay_base[l], _pad_lora(decay_up[l]), iclr_base[l], _pad_lora(iclr_up[l]),
            gate_up[l].astype(BF16), row(k_k[l]), row(k_a[l]), row(r_k[l]))
        y = _rwkv_scan(r, vv, kk, lw, kd, bb, n_ctx)
        xs = _mixer_out(y, bonus, gate, attn, xs, row(ln_x_w[l]), row(ln_x_b[l]), w_out[l].astype(BF16), mod)
        ht, s1, s2 = _peer_scores(xs, mod, row(norm_ffn[l]), peer_query[l].astype(BF16),
                                  peer_subkeys1[l].astype(BF16), peer_subkeys2[l].astype(BF16))
        thr, lse = _peer_topk(s1, s2)
        xs = _peer_experts(ht, s1, s2, thr, lse, expert_u[l].astype(BF16), expert_v[l].T.astype(BF16), xs, mod)
    return xs[:, n_ctx:, :]
```

```python
import functools
import math

import jax
import jax.numpy as jnp
from jax import lax
from jax.experimental import pallas as pl
from jax.experimental.pallas import tpu as pltpu

F32, BF16 = jnp.float32, jnp.bfloat16
HIGHEST = lax.Precision.HIGHEST

HEAD_DIM = 64
ATTN_HEADS = 8
ATTN_KV_HEADS = 2
ATTN_GROUP = ATTN_HEADS // ATTN_KV_HEADS
ATTN_DIM = ATTN_HEADS * HEAD_DIM
KV_DIM = ATTN_KV_HEADS * HEAD_DIM
ATTN_COLS = ATTN_DIM + 2 * KV_DIM
GRID_W = 64
ROPE_THETA = 10000.0
AXIS_ROT_DIM = HEAD_DIM // 2
AXIS_FREQS = AXIS_ROT_DIM // 2
RWKV_DIM = 512
LORA_COLS = 128
N_DIRS = 2
RWKV_COLS = 3 * RWKV_DIM + 3 * LORA_COLS
N_KEYS = 128
PEER_HEADS = 8
PEER_HALF = 128
PEER_TOPK = 16
N_MOD = 6
NORM_EPS = 1e-6
GN_EPS = 64e-5
L2_EPS = 1e-12

LANES = 128
TOKEN_TILE = 256
KV_TILE = 2816
KEY_SUB = 256
ONES_ROWS = 16
CHUNK = 64
SCAN_STEP = 256
EXPERT_TILE = 2048
EXPERT_SUB = 256
VMEM_LIMIT = 48 * 1024 * 1024


def _params(sem):
    return pltpu.CompilerParams(dimension_semantics=sem, vmem_limit_bytes=VMEM_LIMIT)


def _iota(shape, axis):
    return lax.broadcasted_iota(jnp.int32, shape, axis)


def _head_ones(n):
    return ((_iota((n, n), 0) >> 6) == (_iota((n, n), 1) >> 6)).astype(BF16)


def _head_sum(x, ones):
    n = ones.shape[0]
    outs = []
    for c0 in range(0, x.shape[1], n):
        rest = x[:, c0:c0 + n]
        acc = None
        for _ in range(3):
            part = rest.astype(BF16)
            rest = rest - part.astype(F32)
            t = jnp.dot(part, ones, preferred_element_type=F32)
            acc = t if acc is None else acc + t
        outs.append(acc)
    return outs[0] if len(outs) == 1 else jnp.concatenate(outs, axis=1)


def _norm_mod(x, gain, shift, scale):
    ms = jnp.mean(x * x, axis=-1, keepdims=True)
    return (x * lax.rsqrt(ms + NORM_EPS) * gain) * (1.0 + scale) + shift


def _mod_kernel(c_ref, w_ref, b_ref, o_ref):
    c = c_ref[...]
    a = c * jax.nn.sigmoid(c)
    o_ref[...] = jnp.dot(a, w_ref[...], preferred_element_type=F32, precision=HIGHEST) + b_ref[...]


def _modulation(cc, mod_w, mod_b):
    depth, d, n = mod_w.shape
    return pl.pallas_call(
        _mod_kernel,
        grid=(depth, n // d),
        in_specs=[pl.BlockSpec((8, d), lambda l, j: (0, 0)),
                  pl.BlockSpec((None, d, d), lambda l, j: (l, 0, j)),
                  pl.BlockSpec((None, 1, d), lambda l, j: (l, 0, j))],
        out_specs=pl.BlockSpec((None, 8, d), lambda l, j: (l, 0, j)),
        out_shape=jax.ShapeDtypeStruct((depth, 8, n), F32),
        compiler_params=_params(("parallel", "parallel")),
        name="modulation",
    )(cc, mod_w, mod_b.reshape(depth, 1, n))


def _mod_spec(batch, chunk, d):
    return pl.BlockSpec((None, 1, d), lambda b, i, *_: (jnp.where(i == 0, batch, b), 0, chunk))


def _rope(x, cos, sin):
    w = x.shape[1]
    reps = w // LANES
    if reps > 1:
        cos = jnp.concatenate([cos] * reps, axis=1)
        sin = jnp.concatenate([sin] * reps, axis=1)
    first = (_iota(x.shape, 1) & AXIS_FREQS) == 0
    partner = jnp.where(first, pltpu.roll(x, w - AXIS_FREQS, 1), pltpu.roll(x, AXIS_FREQS, 1))
    return x * cos + partner * sin


def _head_norm(x, gain, ones):
    ms = _head_sum(x * x, ones) * (1.0 / HEAD_DIM)
    return x * lax.rsqrt(ms + NORM_EPS) * gain


def _in_kernel(x_ref, gain_ref, sh_ref, sc_ref, w_ref, qg_ref, kg_ref, cos_ref, sin_ref,
               q_ref, k_ref, vt_ref, rw_ref):
    h = _norm_mod(x_ref[...], gain_ref[...], sh_ref[...], sc_ref[...])
    p = jnp.dot(h.astype(BF16), w_ref[...], preferred_element_type=F32)
    cos, sin = cos_ref[...], sin_ref[...]
    q = _head_norm(p[:, :ATTN_DIM], qg_ref[...], _head_ones(256))
    q_ref[...] = (_rope(q, cos, sin) * (HEAD_DIM ** -0.5)).astype(BF16)
    k = _head_norm(p[:, ATTN_DIM:ATTN_DIM + KV_DIM], kg_ref[...], _head_ones(KV_DIM))
    k_ref[...] = _rope(k, cos, sin).astype(BF16)
    vt_ref[...] = p[:, ATTN_DIM + KV_DIM:ATTN_COLS].T.astype(BF16)
    rw_ref[...] = p[:, ATTN_COLS:]


def _in_proj(xs, mod, gain, w_in, q_gain, k_gain, cos_t, sin_t):
    b, t, d = xs.shape
    tm = TOKEN_TILE
    n = w_in.shape[1]
    tile = lambda w: pl.BlockSpec((None, tm, w), lambda bb, i: (bb, i, 0))
    row = lambda w: pl.BlockSpec((1, w), lambda bb, i: (0, 0))
    return pl.pallas_call(
        _in_kernel,
        grid=(b, t // tm),
        in_specs=[tile(d), row(d), _mod_spec(b, 0, d), _mod_spec(b, 1, d),
                  pl.BlockSpec((d, n), lambda bb, i: (0, 0)), row(ATTN_DIM), row(KV_DIM),
                  pl.BlockSpec((tm, LANES), lambda bb, i: (i, 0)),
                  pl.BlockSpec((tm, LANES), lambda bb, i: (i, 0))],
        out_specs=[tile(ATTN_DIM), tile(KV_DIM), pl.BlockSpec((None, KV_DIM, tm), lambda bb, i: (bb, 0, i)),
                   tile(RWKV_COLS)],
        out_shape=[jax.ShapeDtypeStruct((b, t, ATTN_DIM), BF16),
                   jax.ShapeDtypeStruct((b, t, KV_DIM), BF16),
                   jax.ShapeDtypeStruct((b, KV_DIM, t), BF16),
                   jax.ShapeDtypeStruct((b, t, RWKV_COLS), F32)],
        compiler_params=_params(("parallel", "parallel")),
        name="in_proj",
    )(xs, gain, mod, mod, w_in, q_gain, k_gain, cos_t, sin_t)


def _attn_step(qt_ref, k_ref, vt_ref, m_ref, acc_ref, n_keys):
    groups = range(ATTN_KV_HEADS)
    ones = jnp.ones((ONES_ROWS, KEY_SUB), BF16)
    m = [m_ref[g] for g in groups]
    acc = [acc_ref[g] for g in groups]

    def scores(c):
        kc = k_ref[c * KEY_SUB:(c + 1) * KEY_SUB, :]
        return [jnp.dot(kc, qt_ref[g], preferred_element_type=F32) for g in groups]

    s = scores(0)
    for c in range(n_keys // KEY_SUB):
        nxt = scores(c + 1) if (c + 1) * KEY_SUB < n_keys else None
        for g in groups:
            m_new = jnp.maximum(m[g], jnp.max(s[g], axis=0, keepdims=True))
            alpha = jnp.exp(m[g] - m_new)
            p = jnp.exp(s[g] - m_new).astype(BF16)
            va = jnp.concatenate(
                [vt_ref[g * HEAD_DIM:(g + 1) * HEAD_DIM, c * KEY_SUB:(c + 1) * KEY_SUB], ones], axis=0)
            acc[g] = alpha * acc[g] + jnp.dot(va, p, preferred_element_type=F32)
            m[g] = m_new
        s = nxt
    for g in groups:
        m_ref[g] = m[g]
        acc_ref[g] = acc[g]


def _attn_kernel(q_ref, k_ref, vt_ref, o_ref, qt_ref, m_ref, acc_ref, *, n_ctx, nk):
    qi, kj = pl.program_id(1), pl.program_id(2)
    gw = ATTN_GROUP * HEAD_DIM

    @pl.when(kj == 0)
    def _():
        qt_ref[...] = jnp.zeros(qt_ref.shape, BF16)
        for g in range(ATTN_KV_HEADS):
            qg_t = q_ref[:, g * gw:(g + 1) * gw].astype(F32).T.astype(BF16)
            for h in range(ATTN_GROUP):
                qt_ref[g, g * HEAD_DIM:(g + 1) * HEAD_DIM, h * qg_t.shape[1]:(h + 1) * qg_t.shape[1]] = (
                    qg_t[h * HEAD_DIM:(h + 1) * HEAD_DIM, :])
        m_ref[...] = jnp.full(m_ref.shape, -jnp.inf, F32)
        acc_ref[...] = jnp.zeros(acc_ref.shape, F32)

    @pl.when((qi == 0) & (kj == 0))
    def _():
        _attn_step(qt_ref, k_ref, vt_ref, m_ref, acc_ref, n_ctx)

    @pl.when(qi > 0)
    def _():
        _attn_step(qt_ref, k_ref, vt_ref, m_ref, acc_ref, k_ref.shape[0])

    @pl.when(kj == nk - 1)
    def _():
        tq = q_ref.shape[0]
        for g in range(ATTN_KV_HEADS):
            o = acc_ref[g, :HEAD_DIM, :] / acc_ref[g, HEAD_DIM:HEAD_DIM + 1, :]
            o = jnp.concatenate([o[:, h * tq:(h + 1) * tq] for h in range(ATTN_GROUP)], axis=0)
            o_ref[:, g * gw:(g + 1) * gw] = o.T.astype(BF16)


def _attention(q, k, vt, n_ctx):
    b, t, _ = q.shape
    tq = TOKEN_TILE
    tk = KV_TILE if t % KV_TILE == 0 else t
    nk = t // tk
    assert n_ctx == tq and n_ctx % KEY_SUB == 0 and tk % KEY_SUB == 0
    kblock = lambda i, j: jnp.where(i == 0, 0, j)
    return pl.pallas_call(
        functools.partial(_attn_kernel, n_ctx=n_ctx, nk=nk),
        grid=(b, t // tq, nk),
        in_specs=[pl.BlockSpec((None, tq, ATTN_DIM), lambda bb, i, j: (bb, i, 0)),
                  pl.BlockSpec((None, tk, KV_DIM), lambda bb, i, j: (bb, kblock(i, j), 0)),
                  pl.BlockSpec((None, KV_DIM, tk), lambda bb, i, j: (bb, 0, kblock(i, j)))],
        out_specs=pl.BlockSpec((None, tq, ATTN_DIM), lambda bb, i, j: (bb, i, 0)),
        out_shape=jax.ShapeDtypeStruct((b, t, ATTN_DIM), BF16),
        scratch_shapes=[pltpu.VMEM((ATTN_KV_HEADS, KV_DIM, ATTN_GROUP * tq), BF16),
                        pltpu.VMEM((ATTN_KV_HEADS, 1, ATTN_GROUP * tq), F32),
                        pltpu.VMEM((ATTN_KV_HEADS, HEAD_DIM + ONES_ROWS, ATTN_GROUP * tq), F32)],
        compiler_params=_params(("parallel", "parallel", "arbitrary")),
        name="attention",
    )(q, k, vt)


def _feat_kernel(rw_ref, prev_ref, next_ref, taps_ref, dbase_ref, dup_ref, ibase_ref, iup_ref, gup_ref,
                 kk_par_ref, ka_ref, rk_ref,
                 r_out, v_out, kk_out, gate_out, bonus_out, lw_out, kd_out, bb_out, *, nt):
    i = pl.program_id(1)
    x = rw_ref[...]
    tm = x.shape[0]
    prev_row = jnp.where(i <= 1, 0.0, prev_ref[7:8, :])
    next_row = jnp.where((i == 0) | (i == nt - 1), 0.0, next_ref[0:1, :])
    row = _iota(x.shape, 0)
    xm = jnp.where(row == 0, prev_row, pltpu.roll(x, 1, 0))
    xp = jnp.where(row == tm - 1, next_row, pltpu.roll(x, tm - 1, 0))
    xs = xm * taps_ref[0:1, :] + x * taps_ref[1:2, :] + xp * taps_ref[2:3, :]

    o3 = 3 * RWKV_DIM
    r, k, v = xs[:, :RWKV_DIM], xs[:, RWKV_DIM:2 * RWKV_DIM], xs[:, 2 * RWKV_DIM:o3]
    wd = jnp.tanh(xs[:, o3:o3 + LORA_COLS]).astype(BF16)
    ad = xs[:, o3 + LORA_COLS:o3 + 2 * LORA_COLS].astype(BF16)
    gd = jax.nn.sigmoid(xs[:, o3 + 2 * LORA_COLS:]).astype(BF16)
    ones = _head_ones(256)
    kkr = k * kk_par_ref[...]
    kk = kkr * lax.rsqrt(_head_sum(kkr * kkr, ones) + L2_EPS)
    r_out[...] = r
    v_out[...] = v
    kk_out[...] = kk
    gate_out[...] = jnp.dot(gd, gup_ref[...], preferred_element_type=F32)
    bonus_out[...] = _head_sum(r * k * rk_ref[...], ones) * v
    for d in range(N_DIRS):
        z = dbase_ref[d:d + 1, :] + jnp.dot(wd, dup_ref[d], preferred_element_type=F32)
        lw_out[d] = jax.nn.sigmoid(z) * (-math.exp(-0.5))
        iclr = jax.nn.sigmoid(ibase_ref[d:d + 1, :] + jnp.dot(ad, iup_ref[d], preferred_element_type=F32))
        kd_out[d] = k * (1.0 + (iclr - 1.0) * ka_ref[...])
        bb_out[d] = kk * iclr


def _rwkv_features(rw, taps, dbase, dup, ibase, iup, gup, k_k, k_a, r_k):
    b, t, w = rw.shape
    tm = TOKEN_TILE
    nt = t // tm
    r8 = tm // 8
    full = lambda a: pl.BlockSpec(a.shape, lambda bb, i: (0,) * a.ndim)
    tile = pl.BlockSpec((None, tm, RWKV_DIM), lambda bb, i: (bb, i, 0))
    dtile = pl.BlockSpec((None, N_DIRS, tm, RWKV_DIM), lambda bb, i: (bb, 0, i, 0))
    shared = jax.ShapeDtypeStruct((b, t, RWKV_DIM), F32)
    perdir = jax.ShapeDtypeStruct((b, N_DIRS, t, RWKV_DIM), F32)
    consts = (taps, dbase, dup, ibase, iup, gup, k_k, k_a, r_k)
    return pl.pallas_call(
        functools.partial(_feat_kernel, nt=nt),
        grid=(b, nt),
        in_specs=[pl.BlockSpec((None, tm, w), lambda bb, i: (bb, i, 0)),
                  pl.BlockSpec((None, 8, w), lambda bb, i: (bb, jnp.maximum(i * r8 - 1, 0), 0)),
                  pl.BlockSpec((None, 8, w), lambda bb, i: (bb, jnp.minimum((i + 1) * r8, nt * r8 - 1), 0))]
                 + [full(a) for a in consts],
        out_specs=[tile] * 5 + [dtile] * 3,
        out_shape=[shared] * 5 + [perdir] * 3,
        compiler_params=_params(("parallel", "parallel")),
        name="rwkv_features",
    )(rw, rw, rw, *consts)


def _scan_kernel(r_ref, v_ref, kk_ref, lw_ref, kd_ref, bb_ref, y_ref, st_ref):
    rev = pl.program_id(1) == 1
    c = pl.program_id(2)

    @pl.when(c == 0)
    def _():
        st_ref[...] = jnp.zeros(st_ref.shape, F32)

    ch, pw = CHUNK, LANES
    ti, si = _iota((ch, ch), 0), _iota((ch, ch), 1)
    tri = (jnp.where(rev, si - ti, ti - si) >= 0).astype(F32)
    a, bcol = _iota((pw, pw), 0), _iota((pw, pw), 1)
    same = (a >> 6) == (bcol >> 6)
    ahead = jnp.where(rev, a - bcol, bcol - a)
    strict = same & (ahead > 0)
    incl = same & (ahead >= 0)
    eye = a == bcol
    head0 = _iota((ch, pw), 1) < HEAD_DIM

    def stack(x):
        return jnp.concatenate([jnp.where(head0, x, 0.0), jnp.where(head0, 0.0, x)], axis=0)

    def dup(x):
        return jnp.concatenate([x, x], axis=0)

    def mm(x, y):
        return jnp.dot(x.astype(BF16), y.astype(BF16), preferred_element_type=F32)

    pairs = range(RWKV_DIM // pw)
    n_sub = lw_ref.shape[0] // ch
    rows = [pl.ds(pl.multiple_of(jnp.where(rev, n_sub - 1 - j, j) * ch, ch), ch) for j in range(n_sub)]
    sls = [(rows[j], slice(p * pw, (p + 1) * pw)) for j in range(n_sub) for p in pairs]
    each = lambda f, *cols: [f(*args) for args in zip(*cols)]
    lw = [lw_ref[sl] for sl in sls]
    cum = each(lambda a_: jnp.dot(tri, a_, preferred_element_type=F32, precision=HIGHEST), lw)
    tot = each(lambda a_: jnp.sum(a_, axis=0, keepdims=True), lw)
    w_inv = each(lambda c_: jnp.exp(-c_), cum)
    w_rem = each(lambda t_, c_: jnp.exp(t_ - c_), tot, cum)
    at = each(lambda sl, c_, l_: -kk_ref[sl] * jnp.exp(c_ - l_), sls, cum, lw)
    rt = each(lambda sl, c_: r_ref[sl] * jnp.exp(c_), sls, cum)
    bt = each(lambda sl, w_: bb_ref[sl] * w_, sls, w_inv)
    kt = each(lambda sl, w_: kd_ref[sl] * w_, sls, w_inv)
    bw = each(lambda sl, w_: bb_ref[sl] * w_, sls, w_rem)
    kw = each(lambda sl, w_: kd_ref[sl] * w_, sls, w_rem)

    nt_dims = (((1,), (1,)), ((), ()))
    m = each(lambda b_, k_, a_, r_: lax.dot_general(
        jnp.concatenate([stack(b_), stack(k_)], axis=0).astype(BF16),
        jnp.concatenate([dup(a_), dup(r_)], axis=0).astype(BF16), nt_dims, preferred_element_type=F32),
        bt, kt, at, rt)
    mab = each(lambda m_: jnp.where(strict, m_[:pw, :pw], 0.0), m)
    mbr = each(lambda m_: jnp.where(incl, m_[:pw, pw:], 0.0), m)
    mak = each(lambda m_: jnp.where(strict, m_[pw:, :pw], 0.0), m)
    mkr = each(lambda m_: jnp.where(incl, m_[pw:, pw:], 0.0), m)

    n = mab
    x = each(lambda b_, m_: jnp.concatenate([stack(b_), m_], axis=1), bw, mbr)
    levels = CHUNK.bit_length() - 1
    for lvl in range(levels - 1):
        nx = each(lambda n_, x_: mm(n_, jnp.concatenate([n_, x_], axis=1)), n, x)
        n = [nx_[:, :pw] for nx_ in nx]
        x = each(lambda x_, nx_: x_ + nx_[:, pw:], x, nx)
    x = each(lambda n_, x_: x_ + mm(n_, x_), n, x)
    z = [x_[:, :pw] for x_ in x]
    tm_ = [x_[:, pw:] for x_ in x]

    a2 = each(stack, at)
    v2 = each(lambda sl: stack(v_ref[sl]), sls)
    pt = each(lambda z_, a_, t_: mm(z_.T, a_) + jnp.where(eye, jnp.exp(t_), 0.0), z, a2, tot)
    dd = each(lambda k_, z_, w_: mm(k_, z_) + stack(w_), mak, z, kw)
    qt = each(lambda d_, v_: mm(d_.T, v_), dd, v2)
    g = each(lambda r_, t_, a_: stack(r_) + mm(t_.T, a_), rt, tm_, a2)
    hm = each(lambda kr_, k_, t_: kr_ + mm(k_, t_), mkr, mak, tm_)
    hv = each(lambda h_, v_: mm(h_.T, v_), hm, v2)
    st = [st_ref[p] for p in pairs]
    for j in range(n_sub):
        part = slice(j * len(pairs), (j + 1) * len(pairs))
        ys = each(lambda g_, s_, h_: mm(g_, s_) + h_, g[part], st, hv[part])
        st = each(lambda p_, s_, q_: mm(p_, s_) + q_, pt[part], st, qt[part])
        for p in pairs:
            y_ref[sls[part][p]] = ys[p][:ch] + ys[p][ch:]
    for p in pairs:
        st_ref[p] = st[p]


def _rwkv_scan(r, v, kk, lw, kd, bb, n_ctx):
    b, t, w = r.shape
    assert t % SCAN_STEP == 0 and n_ctx % SCAN_STEP == 0 and SCAN_STEP % CHUNK == 0
    nc, ncx = t // SCAN_STEP, n_ctx // SCAN_STEP

    def chunk(d, c):
        return jnp.where(d == 0, c, jnp.where(c < ncx, ncx - 1 - c, nc - 1 + ncx - c))

    shared = pl.BlockSpec((None, SCAN_STEP, w), lambda bb_, d, c: (bb_, chunk(d, c), 0))
    perdir = pl.BlockSpec((None, None, SCAN_STEP, w), lambda bb_, d, c: (bb_, d, chunk(d, c), 0))
    return pl.pallas_call(
        _scan_kernel,
        grid=(b, N_DIRS, nc),
        in_specs=[shared] * 3 + [perdir] * 3,
        out_specs=perdir,
        out_shape=jax.ShapeDtypeStruct((b, N_DIRS, t, w), F32),
        scratch_shapes=[pltpu.VMEM((w // LANES, LANES, LANES), F32)],
        compiler_params=_params(("parallel", "parallel", "arbitrary")),
        name="rwkv_scan",
    )(r, v, kk, lw, kd, bb)


def _out_kernel(y0_ref, y1_ref, bonus_ref, gate_ref, attn_ref, x_ref, lnw_ref, lnb_ref, w_ref, g_ref, o_ref):
    ones = _head_ones(256)
    y = y0_ref[...] + y1_ref[...]
    yc = y - _head_sum(y, ones) * (1.0 / HEAD_DIM)
    var = _head_sum(yc * yc, ones) * (1.0 / HEAD_DIM)
    yn = yc * lax.rsqrt(var + GN_EPS) * lnw_ref[...] + lnb_ref[...]
    rw = ((yn + bonus_ref[...]) * gate_ref[...]).astype(BF16)
    mix = jnp.dot(jnp.concatenate([attn_ref[...], rw], axis=1), w_ref[...], preferred_element_type=F32)
    o_ref[...] = x_ref[...] + g_ref[...] * mix


def _mixer_out(y, bonus, gate, attn, xs, ln_w, ln_b, w_out, mod):
    b, t, d = xs.shape
    tm = TOKEN_TILE
    tile = lambda w: pl.BlockSpec((None, tm, w), lambda bb, i: (bb, i, 0))
    ydir = lambda dd: pl.BlockSpec((None, None, tm, RWKV_DIM), lambda bb, i: (bb, dd, i, 0))
    row = lambda w: pl.BlockSpec((1, w), lambda bb, i: (0, 0))
    return pl.pallas_call(
        _out_kernel,
        grid=(b, t // tm),
        in_specs=[ydir(0), ydir(1), tile(RWKV_DIM), tile(RWKV_DIM), tile(ATTN_DIM), tile(d),
                  row(RWKV_DIM), row(RWKV_DIM), pl.BlockSpec(w_out.shape, lambda bb, i: (0, 0)),
                  _mod_spec(b, 2, d)],
        out_specs=tile(d),
        out_shape=jax.ShapeDtypeStruct((b, t, d), F32),
        compiler_params=_params(("parallel", "parallel")),
        name="mixer_out",
    )(y, y, bonus, gate, attn, xs, ln_w, ln_b, w_out, mod)


def _query_kernel(x_ref, gain_ref, sh_ref, sc_ref, w_ref, k1_ref, k2_ref, ht_ref, s1_ref, s2_ref):
    hf = _norm_mod(x_ref[...], gain_ref[...], sh_ref[...], sc_ref[...])
    ht_ref[...] = hf.T.astype(BF16)
    q = jnp.dot(hf.astype(BF16), w_ref[...], preferred_element_type=F32).astype(BF16)
    nt = (((1,), (1,)), ((), ()))
    for hd in range(PEER_HEADS):
        c0 = hd * 2 * PEER_HALF
        s1_ref[hd] = lax.dot_general(k1_ref[...], q[:, c0:c0 + PEER_HALF], nt, preferred_element_type=F32)
        s2_ref[hd] = lax.dot_general(k2_ref[...], q[:, c0 + PEER_HALF:c0 + 2 * PEER_HALF], nt,
                                     preferred_element_type=F32)


def _peer_scores(xs, mod, gain, w_query, keys1, keys2):
    b, t, d = xs.shape
    tm = TOKEN_TILE
    nt = t // tm
    row = lambda w: pl.BlockSpec((1, w), lambda bb, i: (0, 0))
    full = lambda a: pl.BlockSpec(a.shape, lambda bb, i: (0,) * a.ndim)
    sspec = pl.BlockSpec((PEER_HEADS, N_KEYS, tm), lambda bb, i: (0, 0, bb * nt + i))
    sshape = jax.ShapeDtypeStruct((PEER_HEADS, N_KEYS, b * t), F32)
    return pl.pallas_call(
        _query_kernel,
        grid=(b, nt),
        in_specs=[pl.BlockSpec((None, tm, d), lambda bb, i: (bb, i, 0)), row(d), _mod_spec(b, 3, d),
                  _mod_spec(b, 4, d), full(w_query), full(keys1), full(keys2)],
        out_specs=[pl.BlockSpec((d, tm), lambda bb, i: (0, bb * nt + i)), sspec, sspec],
        out_shape=[jax.ShapeDtypeStruct((d, b * t), BF16), sshape, sshape],
        compiler_params=_params(("parallel", "parallel")),
        name="peer_scores",
    )(xs, gain, mod, mod, w_query, keys1, keys2)


def _top_rows(x_ref, out_ref):
    slabs, n, tl = x_ref.shape
    idx = _iota((n, tl), 0).astype(F32)

    def body(i, carry):
        xs = [x_ref[s] for s in range(slabs)]
        ms = [jnp.max(x, axis=0, keepdims=True) for x in xs]
        firsts = [jnp.min(jnp.where(x == m, idx, float(n)), axis=0, keepdims=True) for x, m in zip(xs, ms)]
        for s in range(slabs):
            out_ref[s, pl.ds(i, 1), :] = ms[s]
            x_ref[s] = jnp.where(idx == firsts[s], -jnp.inf, xs[s])
        return carry

    lax.fori_loop(0, _RANKS, body, 0)


_RANKS = PEER_TOPK + 1
_RANK_ROWS = -(-_RANKS // 8) * 8
_CAND_COUNTS = [_RANKS // (i + 1) for i in range(_RANKS)]
_CAND_ROWS = -(-sum(_CAND_COUNTS) // 8) * 8


def _topk_kernel(s1_ref, s2_ref, thr_ref, lse_ref, work_ref, v_ref, cand_ref, top_ref):
    work_ref[:PEER_HEADS] = s1_ref[...]
    work_ref[PEER_HEADS:] = s2_ref[...]
    _top_rows(work_ref, v_ref)
    cand_ref[...] = jnp.full(cand_ref.shape, -jnp.inf, F32)
    for hd in range(PEER_HEADS):
        off = 0
        for i, cnt in enumerate(_CAND_COUNTS):
            cand_ref[hd, off:off + cnt, :] = v_ref[hd, i:i + 1, :] + v_ref[PEER_HEADS + hd, 0:cnt, :]
            off += cnt
    _top_rows(cand_ref, top_ref)
    for hd in range(PEER_HEADS):
        top = top_ref[hd, 0:PEER_TOPK, :]
        mx = top[0:1, :]
        thr_ref[hd:hd + 1, :] = 0.5 * (top[PEER_TOPK - 1:PEER_TOPK, :] + top_ref[hd, PEER_TOPK:_RANKS, :])
        lse_ref[hd:hd + 1, :] = mx + jnp.log(jnp.sum(jnp.exp(top - mx), axis=0, keepdims=True))


def _peer_topk(s1, s2):
    _, _, n = s1.shape
    tl = TOKEN_TILE
    sspec = pl.BlockSpec((PEER_HEADS, N_KEYS, tl), lambda i: (0, 0, i))
    ospec = pl.BlockSpec((PEER_HEADS, tl), lambda i: (0, i))
    oshape = jax.ShapeDtypeStruct((PEER_HEADS, n), F32)
    return pl.pallas_call(
        _topk_kernel,
        grid=(n // tl,),
        in_specs=[sspec, sspec],
        out_specs=[ospec, ospec],
        out_shape=[oshape, oshape],
        scratch_shapes=[pltpu.VMEM((2 * PEER_HEADS, N_KEYS, tl), F32),
                        pltpu.VMEM((2 * PEER_HEADS, _RANK_ROWS, tl), F32),
                        pltpu.VMEM((PEER_HEADS, _CAND_ROWS, tl), F32),
                        pltpu.VMEM((PEER_HEADS, _RANK_ROWS, tl), F32)],
        compiler_params=_params(("parallel",)),
        name="peer_topk",
    )(s1, s2)


def _peer_kernel(ht_ref, s1_ref, s2_ref, thr_ref, lse_ref, u_ref, vt_ref, x_ref, g_ref, o_ref,
                 acc_ref, need_ref, e2_ref, off_ref, *, n_steps):
    ap = pl.program_id(2)

    @pl.when(ap == 0)
    def _():
        acc_ref[...] = jnp.zeros(acc_ref.shape, F32)
        for hd in range(PEER_HEADS):
            s2 = s2_ref[hd]
            m2 = jnp.max(s2, axis=0, keepdims=True)
            need_ref[hd] = thr_ref[hd:hd + 1, :] - s2
            e2_ref[hd] = jnp.exp(s2 - m2).astype(BF16)
            off_ref[hd:hd + 1, :] = m2 - lse_ref[hd:hd + 1, :]

    ht = ht_ref[...]
    sub, rows = EXPERT_SUB, EXPERT_SUB // N_KEYS

    def pre_act(j):
        return jnp.dot(u_ref[j * sub:(j + 1) * sub, :], ht, preferred_element_type=F32)

    pre, upd, part = pre_act(0), None, None
    for j in range(EXPERT_TILE // sub):
        nxt = pre_act(j + 1) if (j + 1) * sub < EXPERT_TILE else None
        gates = []
        for r in range(rows):
            a = ap * (EXPERT_TILE // N_KEYS) + j * rows + r
            w = None
            for hd in range(PEER_HEADS):
                s1a = s1_ref[hd, pl.ds(a, 1), :]
                f = jnp.exp(s1a + off_ref[hd:hd + 1, :]).astype(BF16)
                e = jnp.where(need_ref[hd] <= s1a, e2_ref[hd] * f, jnp.zeros((), BF16))
                w = e if w is None else w + e
            gates.append(w)
        if part is not None:
            upd = part if upd is None else upd + part
        act = 0.5 * pre * (1.0 + lax.erf(pre * (2.0 ** -0.5)))
        gw = jnp.concatenate(gates, axis=0) * act.astype(BF16)
        part = jnp.dot(vt_ref[:, j * sub:(j + 1) * sub], gw, preferred_element_type=F32)
        pre = nxt
    acc_ref[...] += upd + part

    @pl.when(ap == n_steps - 1)
    def _():
        o_ref[...] = x_ref[...] + g_ref[...] * acc_ref[...].T


def _peer_experts(ht, s1, s2, thr, lse, u, vt, xs, mod, latents_only):
    b, t, d = xs.shape
    tm = TOKEN_TILE
    nt = t // tm
    first = 1 if latents_only else 0
    n_steps = u.shape[0] // EXPERT_TILE
    tok = lambda bb, i, a: bb * nt + i + first
    sspec = pl.BlockSpec((PEER_HEADS, N_KEYS, tm), lambda bb, i, a: (0, 0, tok(bb, i, a)))
    rspec = pl.BlockSpec((PEER_HEADS, tm), lambda bb, i, a: (0, tok(bb, i, a)))
    gate = pl.BlockSpec((None, 1, d), lambda bb, i, a: (jnp.where(i + first == 0, b, bb), 0, N_MOD - 1))
    return pl.pallas_call(
        functools.partial(_peer_kernel, n_steps=n_steps),
        grid=(b, nt - first, n_steps),
        in_specs=[pl.BlockSpec((d, tm), lambda bb, i, a: (0, tok(bb, i, a))), sspec, sspec, rspec, rspec,
                  pl.BlockSpec((EXPERT_TILE, d), lambda bb, i, a: (a, 0)),
                  pl.BlockSpec((d, EXPERT_TILE), lambda bb, i, a: (0, a)),
                  pl.BlockSpec((None, tm, d), lambda bb, i, a: (bb, i + first, 0)), gate],
        out_specs=pl.BlockSpec((None, tm, d), lambda bb, i, a: (bb, i, 0)),
        out_shape=jax.ShapeDtypeStruct((b, t - first * tm, d), F32),
        scratch_shapes=[pltpu.VMEM((d, tm), F32), pltpu.VMEM((PEER_HEADS, N_KEYS, tm), F32),
                        pltpu.VMEM((PEER_HEADS, N_KEYS, tm), BF16), pltpu.VMEM((PEER_HEADS, tm), F32)],
        compiler_params=_params(("parallel", "parallel", "arbitrary")),
        name="peer_experts",
    )(ht, s1, s2, thr, lse, u, vt, xs, mod)


def _rope_tables(n_lat, n_ctx):
    pos = jnp.arange(n_lat)
    inv_freq = ROPE_THETA ** (-jnp.arange(AXIS_FREQS, dtype=F32) * 2.0 / AXIS_ROT_DIM)
    ang = jnp.stack([(pos // GRID_W).astype(F32)[:, None] * inv_freq,
                     (pos % GRID_W).astype(F32)[:, None] * inv_freq], axis=1)
    cos = jnp.broadcast_to(jnp.cos(ang)[:, :, None, :], (n_lat, 2, 2, AXIS_FREQS)).reshape(n_lat, HEAD_DIM)
    sin = jnp.stack([-jnp.sin(ang), jnp.sin(ang)], axis=2).reshape(n_lat, HEAD_DIM)
    cos = jnp.concatenate([jnp.ones((n_ctx, HEAD_DIM), F32), cos], axis=0)
    sin = jnp.concatenate([jnp.zeros((n_ctx, HEAD_DIM), F32), sin], axis=0)
    return jnp.tile(cos, (1, LANES // HEAD_DIM)), jnp.tile(sin, (1, LANES // HEAD_DIM))


def _pad_lora(up):
    z = jnp.zeros_like(up[0])
    return jnp.stack([jnp.concatenate([up[0], z], axis=0), jnp.concatenate([z, up[1]], axis=0)]).astype(BF16)


def kernel(x, c, ctx, c_ctx, mod_w, mod_b, norm_mix, norm_ffn, w_in, q_gain, k_gain, shift_taps, decay_base,
           decay_up, iclr_base, iclr_up, gate_up, k_k, k_a, r_k, ln_x_w, ln_x_b, w_out, peer_query, peer_subkeys1,
           peer_subkeys2, expert_u, expert_v):
    b, n_lat, d = x.shape
    n_ctx = ctx.shape[1]
    depth = mod_w.shape[0]
    assert n_ctx == TOKEN_TILE and n_lat % TOKEN_TILE == 0 and b < 8

    xs = jnp.concatenate([ctx, x], axis=1)
    cc = jnp.concatenate([c, c_ctx[None, :], jnp.zeros((8 - b - 1, d), F32)], axis=0)
    mod_all = _modulation(cc, mod_w, mod_b)
    cos_t, sin_t = _rope_tables(n_lat, n_ctx)
    row = lambda a: a.reshape(1, -1)

    for l in range(depth):
        mod = mod_all[l].reshape(8, 1, N_MOD * d)
        q, k, v, rw = _in_proj(xs, mod, row(norm_mix[l]), w_in[l].astype(BF16),
                               row(jnp.tile(q_gain[l], ATTN_HEADS)), row(jnp.tile(k_gain[l], ATTN_KV_HEADS)),
                               cos_t, sin_t)
        attn = _attention(q, k, v, n_ctx)
        r, vv, kk, gate, bonus, lw, kd, bb = _rwkv_features(
            rw, shift_taps[l], decay_base[l], _pad_lora(decay_up[l]), iclr_base[l], _pad_lora(iclr_up[l]),
            gate_up[l].astype(BF16), row(k_k[l]), row(k_a[l]), row(r_k[l]))
        y = _rwkv_scan(r, vv, kk, lw, kd, bb, n_ctx)
        xs = _mixer_out(y, bonus, gate, attn, xs, row(ln_x_w[l]), row(ln_x_b[l]), w_out[l].astype(BF16), mod)
        ht, s1, s2 = _peer_scores(xs, mod, row(norm_ffn[l]), peer_query[l].astype(BF16),
                                  peer_subkeys1[l].astype(BF16), peer_subkeys2[l].astype(BF16))
        thr, lse = _peer_topk(s1, s2)
        xs = _peer_experts(ht, s1, s2, thr, lse, expert_u[l].astype(BF16), expert_v[l].T.astype(BF16), xs, mod,
                           latents_only=l == depth - 1)
    return xs
```

```python
import functools
import math

import jax
import jax.numpy as jnp
from jax import lax
from jax.experimental import pallas as pl
from jax.experimental.pallas import tpu as pltpu

F32, BF16 = jnp.float32, jnp.bfloat16
HIGHEST = lax.Precision.HIGHEST

HEAD_DIM = 64
ATTN_HEADS = 8
ATTN_KV_HEADS = 2
ATTN_GROUP = ATTN_HEADS // ATTN_KV_HEADS
ATTN_DIM = ATTN_HEADS * HEAD_DIM
KV_DIM = ATTN_KV_HEADS * HEAD_DIM
ATTN_COLS = ATTN_DIM + 2 * KV_DIM
GRID_W = 64
ROPE_THETA = 10000.0
AXIS_ROT_DIM = HEAD_DIM // 2
AXIS_FREQS = AXIS_ROT_DIM // 2
RWKV_DIM = 512
LORA_COLS = 128
N_DIRS = 2
RWKV_COLS = 3 * RWKV_DIM + 3 * LORA_COLS
N_KEYS = 128
PEER_HEADS = 8
PEER_HALF = 128
PEER_TOPK = 16
N_MOD = 6
NORM_EPS = 1e-6
GN_EPS = 64e-5
L2_EPS = 1e-12

LANES = 128
TOKEN_TILE = 256
KV_TILE = 2816
KEY_SUB = 256
ONES_ROWS = 16
CHUNK = 64
SCAN_STEP = 256
EXPERT_TILE = 2048
EXPERT_SUB = 256
VMEM_LIMIT = 48 * 1024 * 1024


def _params(sem):
    return pltpu.CompilerParams(dimension_semantics=sem, vmem_limit_bytes=VMEM_LIMIT)


def _iota(shape, axis):
    return lax.broadcasted_iota(jnp.int32, shape, axis)


def _head_ones(n):
    return ((_iota((n, n), 0) >> 6) == (_iota((n, n), 1) >> 6)).astype(BF16)


def _head_sum(x, ones):
    n = ones.shape[0]
    outs = []
    for c0 in range(0, x.shape[1], n):
        rest = x[:, c0:c0 + n]
        acc = None
        for _ in range(3):
            part = rest.astype(BF16)
            rest = rest - part.astype(F32)
            t = jnp.dot(part, ones, preferred_element_type=F32)
            acc = t if acc is None else acc + t
        outs.append(acc)
    return outs[0] if len(outs) == 1 else jnp.concatenate(outs, axis=1)


def _norm_mod(x, gain, shift, scale):
    ms = jnp.mean(x * x, axis=-1, keepdims=True)
    return (x * lax.rsqrt(ms + NORM_EPS) * gain) * (1.0 + scale) + shift


def _mod_kernel(c_ref, w_ref, b_ref, o_ref):
    c = c_ref[...]
    a = c * jax.nn.sigmoid(c)
    o_ref[...] = jnp.dot(a, w_ref[...], preferred_element_type=F32, precision=HIGHEST) + b_ref[...]


def _modulation(cc, mod_w, mod_b):
    depth, d, n = mod_w.shape
    return pl.pallas_call(
        _mod_kernel,
        grid=(depth, n // d),
        in_specs=[pl.BlockSpec((8, d), lambda l, j: (0, 0)),
                  pl.BlockSpec((None, d, d), lambda l, j: (l, 0, j)),
                  pl.BlockSpec((None, 1, d), lambda l, j: (l, 0, j))],
        out_specs=pl.BlockSpec((None, 8, d), lambda l, j: (l, 0, j)),
        out_shape=jax.ShapeDtypeStruct((depth, 8, n), F32),
        compiler_params=_params(("parallel", "parallel")),
        name="modulation",
    )(cc, mod_w, mod_b.reshape(depth, 1, n))


def _mod_spec(batch, chunk, d):
    return pl.BlockSpec((None, 1, d), lambda b, i, *_: (jnp.where(i == 0, batch, b), 0, chunk))


def _rope(x, cos, sin):
    w = x.shape[1]
    reps = w // LANES
    if reps > 1:
        cos = jnp.concatenate([cos] * reps, axis=1)
        sin = jnp.concatenate([sin] * reps, axis=1)
    first = (_iota(x.shape, 1) & AXIS_FREQS) == 0
    partner = jnp.where(first, pltpu.roll(x, w - AXIS_FREQS, 1), pltpu.roll(x, AXIS_FREQS, 1))
    return x * cos + partner * sin


def _head_norm(x, gain, ones):
    ms = _head_sum(x * x, ones) * (1.0 / HEAD_DIM)
    return x * lax.rsqrt(ms + NORM_EPS) * gain


def _in_kernel(x_ref, gain_ref, sh_ref, sc_ref, w_ref, qg_ref, kg_ref, cos_ref, sin_ref,
               q_ref, k_ref, vt_ref, rw_ref):
    h = _norm_mod(x_ref[...], gain_ref[...], sh_ref[...], sc_ref[...])
    p = jnp.dot(h.astype(BF16), w_ref[...], preferred_element_type=F32)
    cos, sin = cos_ref[...], sin_ref[...]
    q = _head_norm(p[:, :ATTN_DIM], qg_ref[...], _head_ones(256))
    q_ref[...] = (_rope(q, cos, sin) * (HEAD_DIM ** -0.5)).astype(BF16)
    k = _head_norm(p[:, ATTN_DIM:ATTN_DIM + KV_DIM], kg_ref[...], _head_ones(KV_DIM))
    k_ref[...] = _rope(k, cos, sin).astype(BF16)
    vt_ref[...] = p[:, ATTN_DIM + KV_DIM:ATTN_COLS].T.astype(BF16)
    rw_ref[...] = p[:, ATTN_COLS:]


def _in_proj(xs, mod, gain, w_in, q_gain, k_gain, cos_t, sin_t):
    b, t, d = xs.shape
    tm = TOKEN_TILE
    n = w_in.shape[1]
    tile = lambda w: pl.BlockSpec((None, tm, w), lambda bb, i: (bb, i, 0))
    row = lambda w: pl.BlockSpec((1, w), lambda bb, i: (0, 0))
    return pl.pallas_call(
        _in_kernel,
        grid=(b, t // tm),
        in_specs=[tile(d), row(d), _mod_spec(b, 0, d), _mod_spec(b, 1, d),
                  pl.BlockSpec((d, n), lambda bb, i: (0, 0)), row(ATTN_DIM), row(KV_DIM),
                  pl.BlockSpec((tm, LANES), lambda bb, i: (i, 0)),
                  pl.BlockSpec((tm, LANES), lambda bb, i: (i, 0))],
        out_specs=[tile(ATTN_DIM), tile(KV_DIM), pl.BlockSpec((None, KV_DIM, tm), lambda bb, i: (bb, 0, i)),
                   tile(RWKV_COLS)],
        out_shape=[jax.ShapeDtypeStruct((b, t, ATTN_DIM), BF16),
                   jax.ShapeDtypeStruct((b, t, KV_DIM), BF16),
                   jax.ShapeDtypeStruct((b, KV_DIM, t), BF16),
                   jax.ShapeDtypeStruct((b, t, RWKV_COLS), F32)],
        compiler_params=_params(("parallel", "parallel")),
        name="in_proj",
    )(xs, gain, mod, mod, w_in, q_gain, k_gain, cos_t, sin_t)


def _attn_step(qt_ref, k_ref, vt_ref, m_ref, acc_ref, n_keys):
    groups = range(ATTN_KV_HEADS)
    ones = jnp.ones((ONES_ROWS, KEY_SUB), BF16)
    m = [m_ref[g] for g in groups]
    acc = [acc_ref[g] for g in groups]

    def scores(c):
        kc = k_ref[c * KEY_SUB:(c + 1) * KEY_SUB, :]
        return [jnp.dot(kc, qt_ref[g], preferred_element_type=F32) for g in groups]

    s = scores(0)
    for c in range(n_keys // KEY_SUB):
        nxt = scores(c + 1) if (c + 1) * KEY_SUB < n_keys else None
        for g in groups:
            m_new = jnp.maximum(m[g], jnp.max(s[g], axis=0, keepdims=True))
            alpha = jnp.exp(m[g] - m_new)
            p = jnp.exp(s[g] - m_new).astype(BF16)
            va = jnp.concatenate(
                [vt_ref[g * HEAD_DIM:(g + 1) * HEAD_DIM, c * KEY_SUB:(c + 1) * KEY_SUB], ones], axis=0)
            acc[g] = alpha * acc[g] + jnp.dot(va, p, preferred_element_type=F32)
            m[g] = m_new
        s = nxt
    for g in groups:
        m_ref[g] = m[g]
        acc_ref[g] = acc[g]


def _attn_kernel(q_ref, k_ref, vt_ref, o_ref, qt_ref, m_ref, acc_ref, *, n_ctx, nk):
    qi, kj = pl.program_id(1), pl.program_id(2)
    gw = ATTN_GROUP * HEAD_DIM

    @pl.when(kj == 0)
    def _():
        qt_ref[...] = jnp.zeros(qt_ref.shape, BF16)
        for g in range(ATTN_KV_HEADS):
            qg_t = q_ref[:, g * gw:(g + 1) * gw].astype(F32).T.astype(BF16)
            for h in range(ATTN_GROUP):
                qt_ref[g, g * HEAD_DIM:(g + 1) * HEAD_DIM, h * qg_t.shape[1]:(h + 1) * qg_t.shape[1]] = (
                    qg_t[h * HEAD_DIM:(h + 1) * HEAD_DIM, :])
        m_ref[...] = jnp.full(m_ref.shape, -jnp.inf, F32)
        acc_ref[...] = jnp.zeros(acc_ref.shape, F32)

    @pl.when((qi == 0) & (kj == 0))
    def _():
        _attn_step(qt_ref, k_ref, vt_ref, m_ref, acc_ref, n_ctx)

    @pl.when(qi > 0)
    def _():
        _attn_step(qt_ref, k_ref, vt_ref, m_ref, acc_ref, k_ref.shape[0])

    @pl.when(kj == nk - 1)
    def _():
        tq = q_ref.shape[0]
        for g in range(ATTN_KV_HEADS):
            o = acc_ref[g, :HEAD_DIM, :] / acc_ref[g, HEAD_DIM:HEAD_DIM + 1, :]
            o = jnp.concatenate([o[:, h * tq:(h + 1) * tq] for h in range(ATTN_GROUP)], axis=0)
            o_ref[:, g * gw:(g + 1) * gw] = o.T.astype(BF16)


def _attention(q, k, vt, n_ctx):
    b, t, _ = q.shape
    tq = TOKEN_TILE
    tk = KV_TILE if t % KV_TILE == 0 else t
    nk = t // tk
    assert n_ctx == tq and n_ctx % KEY_SUB == 0 and tk % KEY_SUB == 0
    kblock = lambda i, j: jnp.where(i == 0, 0, j)
    return pl.pallas_call(
        functools.partial(_attn_kernel, n_ctx=n_ctx, nk=nk),
        grid=(b, t // tq, nk),
        in_specs=[pl.BlockSpec((None, tq, ATTN_DIM), lambda bb, i, j: (bb, i, 0)),
                  pl.BlockSpec((None, tk, KV_DIM), lambda bb, i, j: (bb, kblock(i, j), 0)),
                  pl.BlockSpec((None, KV_DIM, tk), lambda bb, i, j: (bb, 0, kblock(i, j)))],
        out_specs=pl.BlockSpec((None, tq, ATTN_DIM), lambda bb, i, j: (bb, i, 0)),
        out_shape=jax.ShapeDtypeStruct((b, t, ATTN_DIM), BF16),
        scratch_shapes=[pltpu.VMEM((ATTN_KV_HEADS, KV_DIM, ATTN_GROUP * tq), BF16),
                        pltpu.VMEM((ATTN_KV_HEADS, 1, ATTN_GROUP * tq), F32),
                        pltpu.VMEM((ATTN_KV_HEADS, HEAD_DIM + ONES_ROWS, ATTN_GROUP * tq), F32)],
        compiler_params=_params(("parallel", "parallel", "arbitrary")),
        name="attention",
    )(q, k, vt)


def _feat_kernel(rw_ref, prev_ref, next_ref, taps_ref, dbase_ref, dup_ref, ibase_ref, iup_ref, gup_ref,
                 kk_par_ref, ka_ref, rk_ref,
                 r_out, v_out, kk_out, gate_out, bonus_out, lw_out, kd_out, bb_out, *, nt):
    i = pl.program_id(1)
    x = rw_ref[...]
    tm = x.shape[0]
    prev_row = jnp.where(i <= 1, 0.0, prev_ref[7:8, :])
    next_row = jnp.where((i == 0) | (i == nt - 1), 0.0, next_ref[0:1, :])
    row = _iota(x.shape, 0)
    xm = jnp.where(row == 0, prev_row, pltpu.roll(x, 1, 0))
    xp = jnp.where(row == tm - 1, next_row, pltpu.roll(x, tm - 1, 0))
    xs = xm * taps_ref[0:1, :] + x * taps_ref[1:2, :] + xp * taps_ref[2:3, :]

    o3 = 3 * RWKV_DIM
    r, k, v = xs[:, :RWKV_DIM], xs[:, RWKV_DIM:2 * RWKV_DIM], xs[:, 2 * RWKV_DIM:o3]
    wd = jnp.tanh(xs[:, o3:o3 + LORA_COLS]).astype(BF16)
    ad = xs[:, o3 + LORA_COLS:o3 + 2 * LORA_COLS].astype(BF16)
    gd = jax.nn.sigmoid(xs[:, o3 + 2 * LORA_COLS:]).astype(BF16)
    ones = _head_ones(256)
    kkr = k * kk_par_ref[...]
    kk = kkr * lax.rsqrt(_head_sum(kkr * kkr, ones) + L2_EPS)
    r_out[...] = r
    v_out[...] = v
    kk_out[...] = kk
    gate_out[...] = jnp.dot(gd, gup_ref[...], preferred_element_type=F32)
    bonus_out[...] = _head_sum(r * k * rk_ref[...], ones) * v
    for d in range(N_DIRS):
        z = dbase_ref[d:d + 1, :] + jnp.dot(wd, dup_ref[d], preferred_element_type=F32)
        lw_out[d] = jax.nn.sigmoid(z) * (-math.exp(-0.5))
        iclr = jax.nn.sigmoid(ibase_ref[d:d + 1, :] + jnp.dot(ad, iup_ref[d], preferred_element_type=F32))
        kd_out[d] = k * (1.0 + (iclr - 1.0) * ka_ref[...])
        bb_out[d] = kk * iclr


def _rwkv_features(rw, taps, dbase, dup, ibase, iup, gup, k_k, k_a, r_k):
    b, t, w = rw.shape
    tm = TOKEN_TILE
    nt = t // tm
    r8 = tm // 8
    full = lambda a: pl.BlockSpec(a.shape, lambda bb, i: (0,) * a.ndim)
    tile = pl.BlockSpec((None, tm, RWKV_DIM), lambda bb, i: (bb, i, 0))
    dtile = pl.BlockSpec((None, N_DIRS, tm, RWKV_DIM), lambda bb, i: (bb, 0, i, 0))
    shared = jax.ShapeDtypeStruct((b, t, RWKV_DIM), F32)
    perdir = jax.ShapeDtypeStruct((b, N_DIRS, t, RWKV_DIM), F32)
    consts = (taps, dbase, dup, ibase, iup, gup, k_k, k_a, r_k)
    return pl.pallas_call(
        functools.partial(_feat_kernel, nt=nt),
        grid=(b, nt),
        in_specs=[pl.BlockSpec((None, tm, w), lambda bb, i: (bb, i, 0)),
                  pl.BlockSpec((None, 8, w), lambda bb, i: (bb, jnp.maximum(i * r8 - 1, 0), 0)),
                  pl.BlockSpec((None, 8, w), lambda bb, i: (bb, jnp.minimum((i + 1) * r8, nt * r8 - 1), 0))]
                 + [full(a) for a in consts],
        out_specs=[tile] * 5 + [dtile] * 3,
        out_shape=[shared] * 5 + [perdir] * 3,
        compiler_params=_params(("parallel", "parallel")),
        name="rwkv_features",
    )(rw, rw, rw, *consts)


def _scan_kernel(r_ref, v_ref, kk_ref, lw_ref, kd_ref, bb_ref, y_ref, st_ref):
    rev = pl.program_id(1) == 1
    c = pl.program_id(2)

    @pl.when(c == 0)
    def _():
        st_ref[...] = jnp.zeros(st_ref.shape, F32)

    ch, pw = CHUNK, LANES
    ti, si = _iota((ch, ch), 0), _iota((ch, ch), 1)
    tri = (jnp.where(rev, si - ti, ti - si) >= 0).astype(F32)
    a, bcol = _iota((pw, pw), 0), _iota((pw, pw), 1)
    same = (a >> 6) == (bcol >> 6)
    ahead = jnp.where(rev, a - bcol, bcol - a)
    strict = same & (ahead > 0)
    incl = same & (ahead >= 0)
    eye = a == bcol
    head0 = _iota((ch, pw), 1) < HEAD_DIM

    def stack(x):
        return jnp.concatenate([jnp.where(head0, x, 0.0), jnp.where(head0, 0.0, x)], axis=0)

    def dup(x):
        return jnp.concatenate([x, x], axis=0)

    def mm(x, y):
        return jnp.dot(x.astype(BF16), y.astype(BF16), preferred_element_type=F32)

    pairs = range(RWKV_DIM // pw)
    n_sub = lw_ref.shape[0] // ch
    rows = [pl.ds(pl.multiple_of(jnp.where(rev, n_sub - 1 - j, j) * ch, ch), ch) for j in range(n_sub)]
    sls = [(rows[j], slice(p * pw, (p + 1) * pw)) for j in range(n_sub) for p in pairs]
    each = lambda f, *cols: [f(*args) for args in zip(*cols)]
    lw = [lw_ref[sl] for sl in sls]
    cum = each(lambda a_: jnp.dot(tri, a_, preferred_element_type=F32, precision=HIGHEST), lw)
    tot = each(lambda a_: jnp.sum(a_, axis=0, keepdims=True), lw)
    w_inv = each(lambda c_: jnp.exp(-c_), cum)
    w_rem = each(lambda t_, c_: jnp.exp(t_ - c_), tot, cum)
    at = each(lambda sl, c_, l_: -kk_ref[sl] * jnp.exp(c_ - l_), sls, cum, lw)
    rt = each(lambda sl, c_: r_ref[sl] * jnp.exp(c_), sls, cum)
    bt = each(lambda sl, w_: bb_ref[sl] * w_, sls, w_inv)
    kt = each(lambda sl, w_: kd_ref[sl] * w_, sls, w_inv)
    bw = each(lambda sl, w_: bb_ref[sl] * w_, sls, w_rem)
    kw = each(lambda sl, w_: kd_ref[sl] * w_, sls, w_rem)

    nt_dims = (((1,), (1,)), ((), ()))
    m = each(lambda b_, k_, a_, r_: lax.dot_general(
        jnp.concatenate([stack(b_), stack(k_)], axis=0).astype(BF16),
        jnp.concatenate([dup(a_), dup(r_)], axis=0).astype(BF16), nt_dims, preferred_element_type=F32),
        bt, kt, at, rt)
    mab = each(lambda m_: jnp.where(strict, m_[:pw, :pw], 0.0), m)
    mbr = each(lambda m_: jnp.where(incl, m_[:pw, pw:], 0.0), m)
    mak = each(lambda m_: jnp.where(strict, m_[pw:, :pw], 0.0), m)
    mkr = each(lambda m_: jnp.where(incl, m_[pw:, pw:], 0.0), m)

    levels = CHUNK.bit_length() - 1
    n = each(lambda n_: mm(n_, n_), mab)
    t_inv = each(lambda n_: jnp.where(eye, 1.0, n_), mab)
    for lvl in range(1, levels - 1):
        nx = each(lambda n_, t_: mm(n_, jnp.concatenate([n_, t_], axis=1)), n, t_inv)
        n = [nx_[:, :pw] for nx_ in nx]
        t_inv = each(lambda t_, nx_: t_ + nx_[:, pw:], t_inv, nx)
    t_inv = each(lambda n_, t_: t_ + mm(n_, t_), n, t_inv)
    x = each(lambda t_, b_, m_: mm(t_, jnp.concatenate([stack(b_), m_], axis=1)), t_inv, bw, mbr)
    z = [x_[:, :pw] for x_ in x]
    tm_ = [x_[:, pw:] for x_ in x]

    a2 = each(stack, at)
    v2 = each(lambda sl: stack(v_ref[sl]), sls)
    pt = each(lambda z_, a_, t_: mm(z_.T, a_) + jnp.where(eye, jnp.exp(t_), 0.0), z, a2, tot)
    dd = each(lambda k_, z_, w_: mm(k_, z_) + stack(w_), mak, z, kw)
    qt = each(lambda d_, v_: mm(d_.T, v_), dd, v2)
    g = each(lambda r_, t_, a_: stack(r_) + mm(t_.T, a_), rt, tm_, a2)
    hm = each(lambda kr_, k_, t_: kr_ + mm(k_, t_), mkr, mak, tm_)
    hv = each(lambda h_, v_: mm(h_.T, v_), hm, v2)
    st = [st_ref[p] for p in pairs]
    for j in range(n_sub):
        part = slice(j * len(pairs), (j + 1) * len(pairs))
        ys = each(lambda g_, s_, h_: mm(g_, s_) + h_, g[part], st, hv[part])
        st = each(lambda p_, s_, q_: mm(p_, s_) + q_, pt[part], st, qt[part])
        for p in pairs:
            y_ref[sls[part][p]] = ys[p][:ch] + ys[p][ch:]
    for p in pairs:
        st_ref[p] = st[p]


def _rwkv_scan(r, v, kk, lw, kd, bb, n_ctx):
    b, t, w = r.shape
    assert t % SCAN_STEP == 0 and n_ctx % SCAN_STEP == 0 and SCAN_STEP % CHUNK == 0
    nc, ncx = t // SCAN_STEP, n_ctx // SCAN_STEP

    def chunk(d, c):
        return jnp.where(d == 0, c, jnp.where(c < ncx, ncx - 1 - c, nc - 1 + ncx - c))

    shared = pl.BlockSpec((None, SCAN_STEP, w), lambda bb_, d, c: (bb_, chunk(d, c), 0))
    perdir = pl.BlockSpec((None, None, SCAN_STEP, w), lambda bb_, d, c: (bb_, d, chunk(d, c), 0))
    return pl.pallas_call(
        _scan_kernel,
        grid=(b, N_DIRS, nc),
        in_specs=[shared] * 3 + [perdir] * 3,
        out_specs=perdir,
        out_shape=jax.ShapeDtypeStruct((b, N_DIRS, t, w), F32),
        scratch_shapes=[pltpu.VMEM((w // LANES, LANES, LANES), F32)],
        compiler_params=_params(("parallel", "parallel", "arbitrary")),
        name="rwkv_scan",
    )(r, v, kk, lw, kd, bb)


def _out_kernel(y0_ref, y1_ref, bonus_ref, gate_ref, attn_ref, x_ref, lnw_ref, lnb_ref, w_ref, g_ref, o_ref):
    ones = _head_ones(256)
    y = y0_ref[...] + y1_ref[...]
    yc = y - _head_sum(y, ones) * (1.0 / HEAD_DIM)
    var = _head_sum(yc * yc, ones) * (1.0 / HEAD_DIM)
    yn = yc * lax.rsqrt(var + GN_EPS) * lnw_ref[...] + lnb_ref[...]
    rw = ((yn + bonus_ref[...]) * gate_ref[...]).astype(BF16)
    mix = jnp.dot(jnp.concatenate([attn_ref[...], rw], axis=1), w_ref[...], preferred_element_type=F32)
    o_ref[...] = x_ref[...] + g_ref[...] * mix


def _mixer_out(y, bonus, gate, attn, xs, ln_w, ln_b, w_out, mod):
    b, t, d = xs.shape
    tm = TOKEN_TILE
    tile = lambda w: pl.BlockSpec((None, tm, w), lambda bb, i: (bb, i, 0))
    ydir = lambda dd: pl.BlockSpec((None, None, tm, RWKV_DIM), lambda bb, i: (bb, dd, i, 0))
    row = lambda w: pl.BlockSpec((1, w), lambda bb, i: (0, 0))
    return pl.pallas_call(
        _out_kernel,
        grid=(b, t // tm),
        in_specs=[ydir(0), ydir(1), tile(RWKV_DIM), tile(RWKV_DIM), tile(ATTN_DIM), tile(d),
                  row(RWKV_DIM), row(RWKV_DIM), pl.BlockSpec(w_out.shape, lambda bb, i: (0, 0)),
                  _mod_spec(b, 2, d)],
        out_specs=tile(d),
        out_shape=jax.ShapeDtypeStruct((b, t, d), F32),
        compiler_params=_params(("parallel", "parallel")),
        name="mixer_out",
    )(y, y, bonus, gate, attn, xs, ln_w, ln_b, w_out, mod)


def _query_kernel(x_ref, gain_ref, sh_ref, sc_ref, w_ref, k1_ref, k2_ref, ht_ref, s1_ref, s2_ref):
    hf = _norm_mod(x_ref[...], gain_ref[...], sh_ref[...], sc_ref[...])
    ht_ref[...] = hf.T.astype(BF16)
    q = jnp.dot(hf.astype(BF16), w_ref[...], preferred_element_type=F32).astype(BF16)
    nt = (((1,), (1,)), ((), ()))
    for hd in range(PEER_HEADS):
        c0 = hd * 2 * PEER_HALF
        s1_ref[hd] = lax.dot_general(k1_ref[...], q[:, c0:c0 + PEER_HALF], nt, preferred_element_type=F32)
        s2_ref[hd] = lax.dot_general(k2_ref[...], q[:, c0 + PEER_HALF:c0 + 2 * PEER_HALF], nt,
                                     preferred_element_type=F32)


def _peer_scores(xs, mod, gain, w_query, keys1, keys2):
    b, t, d = xs.shape
    tm = TOKEN_TILE
    nt = t // tm
    row = lambda w: pl.BlockSpec((1, w), lambda bb, i: (0, 0))
    full = lambda a: pl.BlockSpec(a.shape, lambda bb, i: (0,) * a.ndim)
    sspec = pl.BlockSpec((PEER_HEADS, N_KEYS, tm), lambda bb, i: (0, 0, bb * nt + i))
    sshape = jax.ShapeDtypeStruct((PEER_HEADS, N_KEYS, b * t), F32)
    return pl.pallas_call(
        _query_kernel,
        grid=(b, nt),
        in_specs=[pl.BlockSpec((None, tm, d), lambda bb, i: (bb, i, 0)), row(d), _mod_spec(b, 3, d),
                  _mod_spec(b, 4, d), full(w_query), full(keys1), full(keys2)],
        out_specs=[pl.BlockSpec((d, tm), lambda bb, i: (0, bb * nt + i)), sspec, sspec],
        out_shape=[jax.ShapeDtypeStruct((d, b * t), BF16), sshape, sshape],
        compiler_params=_params(("parallel", "parallel")),
        name="peer_scores",
    )(xs, gain, mod, mod, w_query, keys1, keys2)


def _top_rows(x_ref, out_ref):
    slabs, n, tl = x_ref.shape
    idx = _iota((n, tl), 0).astype(F32)

    def body(i, carry):
        xs = [x_ref[s] for s in range(slabs)]
        ms = [jnp.max(x, axis=0, keepdims=True) for x in xs]
        firsts = [jnp.min(jnp.where(x == m, idx, float(n)), axis=0, keepdims=True) for x, m in zip(xs, ms)]
        for s in range(slabs):
            out_ref[s, pl.ds(i, 1), :] = ms[s]
            x_ref[s] = jnp.where(idx == firsts[s], -jnp.inf, xs[s])
        return carry

    lax.fori_loop(0, _RANKS, body, 0)


_RANKS = PEER_TOPK + 1
_RANK_ROWS = -(-_RANKS // 8) * 8
_CAND_COUNTS = [_RANKS // (i + 1) for i in range(_RANKS)]
_CAND_ROWS = -(-sum(_CAND_COUNTS) // 8) * 8


def _topk_kernel(s1_ref, s2_ref, thr_ref, lse_ref, work_ref, v_ref, cand_ref, top_ref):
    work_ref[:PEER_HEADS] = s1_ref[...]
    work_ref[PEER_HEADS:] = s2_ref[...]
    _top_rows(work_ref, v_ref)
    cand_ref[...] = jnp.full(cand_ref.shape, -jnp.inf, F32)
    for hd in range(PEER_HEADS):
        off = 0
        for i, cnt in enumerate(_CAND_COUNTS):
            cand_ref[hd, off:off + cnt, :] = v_ref[hd, i:i + 1, :] + v_ref[PEER_HEADS + hd, 0:cnt, :]
            off += cnt
    _top_rows(cand_ref, top_ref)
    for hd in range(PEER_HEADS):
        top = top_ref[hd, 0:PEER_TOPK, :]
        mx = top[0:1, :]
        thr_ref[hd:hd + 1, :] = 0.5 * (top[PEER_TOPK - 1:PEER_TOPK, :] + top_ref[hd, PEER_TOPK:_RANKS, :])
        lse_ref[hd:hd + 1, :] = mx + jnp.log(jnp.sum(jnp.exp(top - mx), axis=0, keepdims=True))


def _peer_topk(s1, s2):
    _, _, n = s1.shape
    tl = TOKEN_TILE
    sspec = pl.BlockSpec((PEER_HEADS, N_KEYS, tl), lambda i: (0, 0, i))
    ospec = pl.BlockSpec((PEER_HEADS, tl), lambda i: (0, i))
    oshape = jax.ShapeDtypeStruct((PEER_HEADS, n), F32)
    return pl.pallas_call(
        _topk_kernel,
        grid=(n // tl,),
        in_specs=[sspec, sspec],
        out_specs=[ospec, ospec],
        out_shape=[oshape, oshape],
        scratch_shapes=[pltpu.VMEM((2 * PEER_HEADS, N_KEYS, tl), F32),
                        pltpu.VMEM((2 * PEER_HEADS, _RANK_ROWS, tl), F32),
                        pltpu.VMEM((PEER_HEADS, _CAND_ROWS, tl), F32),
                        pltpu.VMEM((PEER_HEADS, _RANK_ROWS, tl), F32)],
        compiler_params=_params(("parallel",)),
        name="peer_topk",
    )(s1, s2)


def _peer_kernel(ht_ref, s1_ref, s2_ref, thr_ref, lse_ref, u_ref, vt_ref, x_ref, g_ref, o_ref,
                 acc_ref, need_ref, e2_ref, off_ref, *, n_steps):
    ap = pl.program_id(2)

    @pl.when(ap == 0)
    def _():
        acc_ref[...] = jnp.zeros(acc_ref.shape, F32)
        for hd in range(PEER_HEADS):
            s2 = s2_ref[hd]
            m2 = jnp.max(s2, axis=0, keepdims=True)
            need_ref[hd] = thr_ref[hd:hd + 1, :] - s2
            e2_ref[hd] = jnp.exp(s2 - m2).astype(BF16)
            off_ref[hd:hd + 1, :] = m2 - lse_ref[hd:hd + 1, :]

    ht = ht_ref[...]
    sub, rows = EXPERT_SUB, EXPERT_SUB // N_KEYS

    def pre_act(j):
        return jnp.dot(u_ref[j * sub:(j + 1) * sub, :], ht, preferred_element_type=F32)

    pre, upd, part = pre_act(0), None, None
    for j in range(EXPERT_TILE // sub):
        nxt = pre_act(j + 1) if (j + 1) * sub < EXPERT_TILE else None
        gates = []
        for r in range(rows):
            a = ap * (EXPERT_TILE // N_KEYS) + j * rows + r
            w = None
            for hd in range(PEER_HEADS):
                s1a = s1_ref[hd, pl.ds(a, 1), :]
                f = jnp.exp(s1a + off_ref[hd:hd + 1, :]).astype(BF16)
                e = jnp.where(need_ref[hd] <= s1a, e2_ref[hd] * f, jnp.zeros((), BF16))
                w = e if w is None else w + e
            gates.append(w)
        if part is not None:
            upd = part if upd is None else upd + part
        act = 0.5 * pre * (1.0 + lax.erf(pre * (2.0 ** -0.5)))
        gw = jnp.concatenate(gates, axis=0) * act.astype(BF16)
        part = jnp.dot(vt_ref[:, j * sub:(j + 1) * sub], gw, preferred_element_type=F32)
        pre = nxt
    acc_ref[...] += upd + part

    @pl.when(ap == n_steps - 1)
    def _():
        o_ref[...] = x_ref[...] + g_ref[...] * acc_ref[...].T


def _peer_experts(ht, s1, s2, thr, lse, u, vt, xs, mod, latents_only):
    b, t, d = xs.shape
    tm = TOKEN_TILE
    nt = t // tm
    first = 1 if latents_only else 0
    n_steps = u.shape[0] // EXPERT_TILE
    tok = lambda bb, i, a: bb * nt + i + first
    sspec = pl.BlockSpec((PEER_HEADS, N_KEYS, tm), lambda bb, i, a: (0, 0, tok(bb, i, a)))
    rspec = pl.BlockSpec((PEER_HEADS, tm), lambda bb, i, a: (0, tok(bb, i, a)))
    gate = pl.BlockSpec((None, 1, d), lambda bb, i, a: (jnp.where(i + first == 0, b, bb), 0, N_MOD - 1))
    return pl.pallas_call(
        functools.partial(_peer_kernel, n_steps=n_steps),
        grid=(b, nt - first, n_steps),
        in_specs=[pl.BlockSpec((d, tm), lambda bb, i, a: (0, tok(bb, i, a))), sspec, sspec, rspec, rspec,
                  pl.BlockSpec((EXPERT_TILE, d), lambda bb, i, a: (a, 0)),
                  pl.BlockSpec((d, EXPERT_TILE), lambda bb, i, a: (0, a)),
                  pl.BlockSpec((None, tm, d), lambda bb, i, a: (bb, i + first, 0)), gate],
        out_specs=pl.BlockSpec((None, tm, d), lambda bb, i, a: (bb, i, 0)),
        out_shape=jax.ShapeDtypeStruct((b, t - first * tm, d), F32),
        scratch_shapes=[pltpu.VMEM((d, tm), F32), pltpu.VMEM((PEER_HEADS, N_KEYS, tm), F32),
                        pltpu.VMEM((PEER_HEADS, N_KEYS, tm), BF16), pltpu.VMEM((PEER_HEADS, tm), F32)],
        compiler_params=_params(("parallel", "parallel", "arbitrary")),
        name="peer_experts",
    )(ht, s1, s2, thr, lse, u, vt, xs, mod)


def _rope_tables(n_lat, n_ctx):
    pos = jnp.arange(n_lat)
    inv_freq = ROPE_THETA ** (-jnp.arange(AXIS_FREQS, dtype=F32) * 2.0 / AXIS_ROT_DIM)
    ang = jnp.stack([(pos // GRID_W).astype(F32)[:, None] * inv_freq,
                     (pos % GRID_W).astype(F32)[:, None] * inv_freq], axis=1)
    cos = jnp.broadcast_to(jnp.cos(ang)[:, :, None, :], (n_lat, 2, 2, AXIS_FREQS)).reshape(n_lat, HEAD_DIM)
    sin = jnp.stack([-jnp.sin(ang), jnp.sin(ang)], axis=2).reshape(n_lat, HEAD_DIM)
    cos = jnp.concatenate([jnp.ones((n_ctx, HEAD_DIM), F32), cos], axis=0)
    sin = jnp.concatenate([jnp.zeros((n_ctx, HEAD_DIM), F32), sin], axis=0)
    return jnp.tile(cos, (1, LANES // HEAD_DIM)), jnp.tile(sin, (1, LANES // HEAD_DIM))


def _pad_lora(up):
    z = jnp.zeros_like(up[0])
    return jnp.stack([jnp.concatenate([up[0], z], axis=0), jnp.concatenate([z, up[1]], axis=0)]).astype(BF16)


def kernel(x, c, ctx, c_ctx, mod_w, mod_b, norm_mix, norm_ffn, w_in, q_gain, k_gain, shift_taps, decay_base,
           decay_up, iclr_base, iclr_up, gate_up, k_k, k_a, r_k, ln_x_w, ln_x_b, w_out, peer_query, peer_subkeys1,
           peer_subkeys2, expert_u, expert_v):
    b, n_lat, d = x.shape
    n_ctx = ctx.shape[1]
    depth = mod_w.shape[0]
    assert n_ctx == TOKEN_TILE and n_lat % TOKEN_TILE == 0 and b < 8

    xs = jnp.concatenate([ctx, x], axis=1)
    cc = jnp.concatenate([c, c_ctx[None, :], jnp.zeros((8 - b - 1, d), F32)], axis=0)
    mod_all = _modulation(cc, mod_w, mod_b)
    cos_t, sin_t = _rope_tables(n_lat, n_ctx)
    row = lambda a: a.reshape(1, -1)

    for l in range(depth):
        mod = mod_all[l].reshape(8, 1, N_MOD * d)
        q, k, v, rw = _in_proj(xs, mod, row(norm_mix[l]), w_in[l].astype(BF16),
                               row(jnp.tile(q_gain[l], ATTN_HEADS)), row(jnp.tile(k_gain[l], ATTN_KV_HEADS)),
                               cos_t, sin_t)
        attn = _attention(q, k, v, n_ctx)
        r, vv, kk, gate, bonus, lw, kd, bb = _rwkv_features(
            rw, shift_taps[l], decay_base[l], _pad_lora(decay_up[l]), iclr_base[l], _pad_lora(iclr_up[l]),
            gate_up[l].astype(BF16), row(k_k[l]), row(k_a[l]), row(r_k[l]))
        y = _rwkv_scan(r, vv, kk, lw, kd, bb, n_ctx)
        xs = _mixer_out(y, bonus, gate, attn, xs, row(ln_x_w[l]), row(ln_x_b[l]), w_out[l].astype(BF16), mod)
        ht, s1, s2 = _peer_scores(xs, mod, row(norm_ffn[l]), peer_query[l].astype(BF16),
                                  peer_subkeys1[l].astype(BF16), peer_subkeys2[l].astype(BF16))
        thr, lse = _peer_topk(s1, s2)
        xs = _peer_experts(ht, s1, s2, thr, lse, expert_u[l].astype(BF16), expert_v[l].T.astype(BF16), xs, mod,
                           latents_only=l == depth - 1)
    return xs
```

```python
import functools
import math

import jax
import jax.numpy as jnp
from jax import lax
from jax.experimental import pallas as pl
from jax.experimental.pallas import tpu as pltpu

F32, BF16 = jnp.float32, jnp.bfloat16
HIGHEST = lax.Precision.HIGHEST

HEAD_DIM = 64
ATTN_HEADS = 8
ATTN_KV_HEADS = 2
ATTN_GROUP = ATTN_HEADS // ATTN_KV_HEADS
ATTN_DIM = ATTN_HEADS * HEAD_DIM
KV_DIM = ATTN_KV_HEADS * HEAD_DIM
ATTN_COLS = ATTN_DIM + 2 * KV_DIM
GRID_W = 64
ROPE_THETA = 10000.0
AXIS_ROT_DIM = HEAD_DIM // 2
AXIS_FREQS = AXIS_ROT_DIM // 2
RWKV_DIM = 512
LORA_COLS = 128
N_DIRS = 2
RWKV_COLS = 3 * RWKV_DIM + 3 * LORA_COLS
N_KEYS = 128
PEER_HEADS = 8
PEER_HALF = 128
PEER_TOPK = 16
N_MOD = 6
NORM_EPS = 1e-6
GN_EPS = 64e-5
L2_EPS = 1e-12

LANES = 128
SUBLANES = 8
MXU_WIDTH = 256
HEAD_SHIFT = HEAD_DIM.bit_length() - 1
TOKEN_TILE = 256
KV_TILE = 2816
KEY_SUB = 256
ONES_ROWS = 16
CHUNK = 64
SCAN_STEP = 256
EXPERT_TILE = 2048
EXPERT_SUB = 256
VMEM_LIMIT = 48 * 1024 * 1024


def _params(sem):
    return pltpu.CompilerParams(dimension_semantics=sem, vmem_limit_bytes=VMEM_LIMIT)


def _iota(shape, axis):
    return lax.broadcasted_iota(jnp.int32, shape, axis)


def _head_ones(n):
    return ((_iota((n, n), 0) >> HEAD_SHIFT) == (_iota((n, n), 1) >> HEAD_SHIFT)).astype(BF16)


def _head_sum(x, ones):
    n = ones.shape[0]
    outs = []
    for c0 in range(0, x.shape[1], n):
        rest = x[:, c0:c0 + n]
        acc = None
        for _ in range(3):
            part = rest.astype(BF16)
            rest = rest - part.astype(F32)
            t = jnp.dot(part, ones, preferred_element_type=F32)
            acc = t if acc is None else acc + t
        outs.append(acc)
    return outs[0] if len(outs) == 1 else jnp.concatenate(outs, axis=1)


def _norm_mod(x, gain, shift, scale):
    ms = jnp.mean(x * x, axis=-1, keepdims=True)
    return (x * lax.rsqrt(ms + NORM_EPS) * gain) * (1.0 + scale) + shift


def _mod_kernel(c_ref, w_ref, b_ref, o_ref):
    c = c_ref[...]
    a = c * jax.nn.sigmoid(c)
    o_ref[...] = jnp.dot(a, w_ref[...], preferred_element_type=F32, precision=HIGHEST) + b_ref[...]


def _modulation(cc, mod_w, mod_b):
    depth, d, n = mod_w.shape
    return pl.pallas_call(
        _mod_kernel,
        grid=(depth, n // d),
        in_specs=[pl.BlockSpec((SUBLANES, d), lambda l, j: (0, 0)),
                  pl.BlockSpec((None, d, d), lambda l, j: (l, 0, j)),
                  pl.BlockSpec((None, 1, d), lambda l, j: (l, 0, j))],
        out_specs=pl.BlockSpec((None, SUBLANES, d), lambda l, j: (l, 0, j)),
        out_shape=jax.ShapeDtypeStruct((depth, SUBLANES, n), F32),
        compiler_params=_params(("parallel", "parallel")),
        name="modulation",
    )(cc, mod_w, mod_b.reshape(depth, 1, n))


def _mod_spec(batch, chunk, d):
    return pl.BlockSpec((None, 1, d), lambda b, i, *_: (jnp.where(i == 0, batch, b), 0, chunk))


def _rope(x, cos, sin):
    w = x.shape[1]
    reps = w // LANES
    if reps > 1:
        cos = jnp.concatenate([cos] * reps, axis=1)
        sin = jnp.concatenate([sin] * reps, axis=1)
    first = (_iota(x.shape, 1) & AXIS_FREQS) == 0
    partner = jnp.where(first, pltpu.roll(x, w - AXIS_FREQS, 1), pltpu.roll(x, AXIS_FREQS, 1))
    return x * cos + partner * sin


def _head_norm(x, gain, ones):
    ms = _head_sum(x * x, ones) * (1.0 / HEAD_DIM)
    return x * lax.rsqrt(ms + NORM_EPS) * gain


def _in_kernel(x_ref, gain_ref, sh_ref, sc_ref, w_ref, qg_ref, kg_ref, cos_ref, sin_ref,
               q_ref, k_ref, vt_ref, rw_ref):
    h = _norm_mod(x_ref[...], gain_ref[...], sh_ref[...], sc_ref[...])
    p = jnp.dot(h.astype(BF16), w_ref[...], preferred_element_type=F32)
    cos, sin = cos_ref[...], sin_ref[...]
    q = _head_norm(p[:, :ATTN_DIM], qg_ref[...], _head_ones(MXU_WIDTH))
    q_ref[...] = (_rope(q, cos, sin) * (HEAD_DIM ** -0.5)).astype(BF16)
    k = _head_norm(p[:, ATTN_DIM:ATTN_DIM + KV_DIM], kg_ref[...], _head_ones(KV_DIM))
    k_ref[...] = _rope(k, cos, sin).astype(BF16)
    vt_ref[...] = p[:, ATTN_DIM + KV_DIM:ATTN_COLS].T.astype(BF16)
    rw_ref[...] = p[:, ATTN_COLS:]


def _in_proj(xs, mod, gain, w_in, q_gain, k_gain, cos_t, sin_t):
    b, t, d = xs.shape
    tm = TOKEN_TILE
    n = w_in.shape[1]
    tile = lambda w: pl.BlockSpec((None, tm, w), lambda bb, i: (bb, i, 0))
    row = lambda w: pl.BlockSpec((1, w), lambda bb, i: (0, 0))
    return pl.pallas_call(
        _in_kernel,
        grid=(b, t // tm),
        in_specs=[tile(d), row(d), _mod_spec(b, 0, d), _mod_spec(b, 1, d),
                  pl.BlockSpec((d, n), lambda bb, i: (0, 0)), row(ATTN_DIM), row(KV_DIM),
                  pl.BlockSpec((tm, LANES), lambda bb, i: (i, 0)),
                  pl.BlockSpec((tm, LANES), lambda bb, i: (i, 0))],
        out_specs=[tile(ATTN_DIM), tile(KV_DIM), pl.BlockSpec((None, KV_DIM, tm), lambda bb, i: (bb, 0, i)),
                   tile(RWKV_COLS)],
        out_shape=[jax.ShapeDtypeStruct((b, t, ATTN_DIM), BF16),
                   jax.ShapeDtypeStruct((b, t, KV_DIM), BF16),
                   jax.ShapeDtypeStruct((b, KV_DIM, t), BF16),
                   jax.ShapeDtypeStruct((b, t, RWKV_COLS), F32)],
        compiler_params=_params(("parallel", "parallel")),
        name="in_proj",
    )(xs, gain, mod, mod, w_in, q_gain, k_gain, cos_t, sin_t)


def _attn_step(qt_ref, k_ref, vt_ref, m_ref, acc_ref, n_keys):
    groups = range(ATTN_KV_HEADS)
    ones = jnp.ones((ONES_ROWS, KEY_SUB), BF16)
    m = [m_ref[g] for g in groups]
    acc = [acc_ref[g] for g in groups]

    def scores(c):
        kc = k_ref[c * KEY_SUB:(c + 1) * KEY_SUB, :]
        return [jnp.dot(kc, qt_ref[g], preferred_element_type=F32) for g in groups]

    s = scores(0)
    for c in range(n_keys // KEY_SUB):
        nxt = scores(c + 1) if (c + 1) * KEY_SUB < n_keys else None
        for g in groups:
            m_new = jnp.maximum(m[g], jnp.max(s[g], axis=0, keepdims=True))
            alpha = jnp.exp(m[g] - m_new)
            p = jnp.exp(s[g] - m_new).astype(BF16)
            va = jnp.concatenate(
                [vt_ref[g * HEAD_DIM:(g + 1) * HEAD_DIM, c * KEY_SUB:(c + 1) * KEY_SUB], ones], axis=0)
            acc[g] = alpha * acc[g] + jnp.dot(va, p, preferred_element_type=F32)
            m[g] = m_new
        s = nxt
    for g in groups:
        m_ref[g] = m[g]
        acc_ref[g] = acc[g]


def _attn_kernel(q_ref, k_ref, vt_ref, o_ref, qt_ref, m_ref, acc_ref, *, n_ctx, nk):
    qi, kj = pl.program_id(1), pl.program_id(2)
    gw = ATTN_GROUP * HEAD_DIM

    @pl.when(kj == 0)
    def _():
        qt_ref[...] = jnp.zeros(qt_ref.shape, BF16)
        for g in range(ATTN_KV_HEADS):
            qg_t = q_ref[:, g * gw:(g + 1) * gw].astype(F32).T.astype(BF16)
            for h in range(ATTN_GROUP):
                qt_ref[g, g * HEAD_DIM:(g + 1) * HEAD_DIM, h * qg_t.shape[1]:(h + 1) * qg_t.shape[1]] = (
                    qg_t[h * HEAD_DIM:(h + 1) * HEAD_DIM, :])
        m_ref[...] = jnp.full(m_ref.shape, -jnp.inf, F32)
        acc_ref[...] = jnp.zeros(acc_ref.shape, F32)

    @pl.when((qi == 0) & (kj == 0))
    def _():
        _attn_step(qt_ref, k_ref, vt_ref, m_ref, acc_ref, n_ctx)

    @pl.when(qi > 0)
    def _():
        _attn_step(qt_ref, k_ref, vt_ref, m_ref, acc_ref, k_ref.shape[0])

    @pl.when(kj == nk - 1)
    def _():
        tq = q_ref.shape[0]
        for g in range(ATTN_KV_HEADS):
            o = acc_ref[g, :HEAD_DIM, :] / acc_ref[g, HEAD_DIM:HEAD_DIM + 1, :]
            o = jnp.concatenate([o[:, h * tq:(h + 1) * tq] for h in range(ATTN_GROUP)], axis=0)
            o_ref[:, g * gw:(g + 1) * gw] = o.T.astype(BF16)


def _attention(q, k, vt, n_ctx):
    b, t, _ = q.shape
    tq = TOKEN_TILE
    tk = KV_TILE if t % KV_TILE == 0 else t
    nk = t // tk
    assert n_ctx == tq and n_ctx % KEY_SUB == 0 and tk % KEY_SUB == 0
    kblock = lambda i, j: jnp.where(i == 0, 0, j)
    return pl.pallas_call(
        functools.partial(_attn_kernel, n_ctx=n_ctx, nk=nk),
        grid=(b, t // tq, nk),
        in_specs=[pl.BlockSpec((None, tq, ATTN_DIM), lambda bb, i, j: (bb, i, 0)),
                  pl.BlockSpec((None, tk, KV_DIM), lambda bb, i, j: (bb, kblock(i, j), 0)),
                  pl.BlockSpec((None, KV_DIM, tk), lambda bb, i, j: (bb, 0, kblock(i, j)))],
        out_specs=pl.BlockSpec((None, tq, ATTN_DIM), lambda bb, i, j: (bb, i, 0)),
        out_shape=jax.ShapeDtypeStruct((b, t, ATTN_DIM), BF16),
        scratch_shapes=[pltpu.VMEM((ATTN_KV_HEADS, KV_DIM, ATTN_GROUP * tq), BF16),
                        pltpu.VMEM((ATTN_KV_HEADS, 1, ATTN_GROUP * tq), F32),
                        pltpu.VMEM((ATTN_KV_HEADS, HEAD_DIM + ONES_ROWS, ATTN_GROUP * tq), F32)],
        compiler_params=_params(("parallel", "parallel", "arbitrary")),
        name="attention",
    )(q, k, vt)


def _feat_kernel(rw_ref, prev_ref, next_ref, taps_ref, dbase_ref, dup_ref, ibase_ref, iup_ref, gup_ref,
                 kk_par_ref, ka_ref, rk_ref,
                 r_out, v_out, kk_out, gate_out, bonus_out, lw_out, kd_out, bb_out, *, nt):
    i = pl.program_id(1)
    x = rw_ref[...]
    tm = x.shape[0]
    prev_row = jnp.where(i <= 1, 0.0, prev_ref[SUBLANES - 1:SUBLANES, :])
    next_row = jnp.where((i == 0) | (i == nt - 1), 0.0, next_ref[0:1, :])
    row = _iota(x.shape, 0)
    xm = jnp.where(row == 0, prev_row, pltpu.roll(x, 1, 0))
    xp = jnp.where(row == tm - 1, next_row, pltpu.roll(x, tm - 1, 0))
    xs = xm * taps_ref[0:1, :] + x * taps_ref[1:2, :] + xp * taps_ref[2:3, :]

    o3 = 3 * RWKV_DIM
    r, k, v = xs[:, :RWKV_DIM], xs[:, RWKV_DIM:2 * RWKV_DIM], xs[:, 2 * RWKV_DIM:o3]
    wd = jnp.tanh(xs[:, o3:o3 + LORA_COLS]).astype(BF16)
    ad = xs[:, o3 + LORA_COLS:o3 + 2 * LORA_COLS].astype(BF16)
    gd = jax.nn.sigmoid(xs[:, o3 + 2 * LORA_COLS:]).astype(BF16)
    ones = _head_ones(MXU_WIDTH)
    kkr = k * kk_par_ref[...]
    kk = kkr * lax.rsqrt(_head_sum(kkr * kkr, ones) + L2_EPS)
    r_out[...] = r
    v_out[...] = v
    kk_out[...] = kk
    gate_out[...] = jnp.dot(gd, gup_ref[...], preferred_element_type=F32)
    bonus_out[...] = _head_sum(r * k * rk_ref[...], ones) * v
    for d in range(N_DIRS):
        z = dbase_ref[d:d + 1, :] + jnp.dot(wd, dup_ref[d], preferred_element_type=F32)
        lw_out[d] = jax.nn.sigmoid(z) * (-math.exp(-0.5))
        iclr = jax.nn.sigmoid(ibase_ref[d:d + 1, :] + jnp.dot(ad, iup_ref[d], preferred_element_type=F32))
        kd_out[d] = k * (1.0 + (iclr - 1.0) * ka_ref[...])
        bb_out[d] = kk * iclr


def _rwkv_features(rw, taps, dbase, dup, ibase, iup, gup, k_k, k_a, r_k):
    b, t, w = rw.shape
    tm = TOKEN_TILE
    nt = t // tm
    r8 = tm // SUBLANES
    full = lambda a: pl.BlockSpec(a.shape, lambda bb, i: (0,) * a.ndim)
    tile = pl.BlockSpec((None, tm, RWKV_DIM), lambda bb, i: (bb, i, 0))
    dtile = pl.BlockSpec((None, N_DIRS, tm, RWKV_DIM), lambda bb, i: (bb, 0, i, 0))
    shared = jax.ShapeDtypeStruct((b, t, RWKV_DIM), F32)
    perdir = jax.ShapeDtypeStruct((b, N_DIRS, t, RWKV_DIM), F32)
    consts = (taps, dbase, dup, ibase, iup, gup, k_k, k_a, r_k)
    return pl.pallas_call(
        functools.partial(_feat_kernel, nt=nt),
        grid=(b, nt),
        in_specs=[pl.BlockSpec((None, tm, w), lambda bb, i: (bb, i, 0)),
                  pl.BlockSpec((None, SUBLANES, w), lambda bb, i: (bb, jnp.maximum(i * r8 - 1, 0), 0)),
                  pl.BlockSpec((None, SUBLANES, w),
                               lambda bb, i: (bb, jnp.minimum((i + 1) * r8, nt * r8 - 1), 0))]
                 + [full(a) for a in consts],
        out_specs=[tile] * 5 + [dtile] * 3,
        out_shape=[shared] * 5 + [perdir] * 3,
        compiler_params=_params(("parallel", "parallel")),
        name="rwkv_features",
    )(rw, rw, rw, *consts)


def _scan_kernel(r_ref, v_ref, kk_ref, lw_ref, kd_ref, bb_ref, y_ref, st_ref):
    rev = pl.program_id(1) == 1
    c = pl.program_id(2)

    @pl.when(c == 0)
    def _():
        st_ref[...] = jnp.zeros(st_ref.shape, F32)

    ch, pw = CHUNK, LANES
    ti, si = _iota((ch, ch), 0), _iota((ch, ch), 1)
    tri = (jnp.where(rev, si - ti, ti - si) >= 0).astype(F32)
    a, bcol = _iota((pw, pw), 0), _iota((pw, pw), 1)
    same = (a >> HEAD_SHIFT) == (bcol >> HEAD_SHIFT)
    ahead = jnp.where(rev, a - bcol, bcol - a)
    strict = same & (ahead > 0)
    incl = same & (ahead >= 0)
    eye = a == bcol
    head0 = _iota((ch, pw), 1) < HEAD_DIM

    def stack(x):
        return jnp.concatenate([jnp.where(head0, x, 0.0), jnp.where(head0, 0.0, x)], axis=0)

    def dup(x):
        return jnp.concatenate([x, x], axis=0)

    def mm(x, y):
        return jnp.dot(x.astype(BF16), y.astype(BF16), preferred_element_type=F32)

    pairs = range(RWKV_DIM // pw)
    n_sub = lw_ref.shape[0] // ch
    rows = [pl.ds(pl.multiple_of(jnp.where(rev, n_sub - 1 - j, j) * ch, ch), ch) for j in range(n_sub)]
    sls = [(rows[j], slice(p * pw, (p + 1) * pw)) for j in range(n_sub) for p in pairs]
    each = lambda f, *cols: [f(*args) for args in zip(*cols)]
    lw = [lw_ref[sl] for sl in sls]
    cum = each(lambda a_: jnp.dot(tri, a_, preferred_element_type=F32, precision=HIGHEST), lw)
    tot = each(lambda a_: jnp.sum(a_, axis=0, keepdims=True), lw)
    w_inv = each(lambda c_: jnp.exp(-c_), cum)
    w_rem = each(lambda t_, c_: jnp.exp(t_ - c_), tot, cum)
    at = each(lambda sl, c_, l_: -kk_ref[sl] * jnp.exp(c_ - l_), sls, cum, lw)
    rt = each(lambda sl, c_: r_ref[sl] * jnp.exp(c_), sls, cum)
    bt = each(lambda sl, w_: bb_ref[sl] * w_, sls, w_inv)
    kt = each(lambda sl, w_: kd_ref[sl] * w_, sls, w_inv)
    bw = each(lambda sl, w_: bb_ref[sl] * w_, sls, w_rem)
    kw = each(lambda sl, w_: kd_ref[sl] * w_, sls, w_rem)

    nt_dims = (((1,), (1,)), ((), ()))
    m = each(lambda b_, k_, a_, r_: lax.dot_general(
        jnp.concatenate([stack(b_), stack(k_)], axis=0).astype(BF16),
        jnp.concatenate([dup(a_), dup(r_)], axis=0).astype(BF16), nt_dims, preferred_element_type=F32),
        bt, kt, at, rt)
    mab = each(lambda m_: jnp.where(strict, m_[:pw, :pw], 0.0), m)
    mbr = each(lambda m_: jnp.where(incl, m_[:pw, pw:], 0.0), m)
    mak = each(lambda m_: jnp.where(strict, m_[pw:, :pw], 0.0), m)
    mkr = each(lambda m_: jnp.where(incl, m_[pw:, pw:], 0.0), m)

    levels = CHUNK.bit_length() - 1
    n = each(lambda n_: mm(n_, n_), mab)
    t_inv = each(lambda n_: jnp.where(eye, 1.0, n_), mab)
    for lvl in range(1, levels - 1):
        nx = each(lambda n_, t_: mm(n_, jnp.concatenate([n_, t_], axis=1)), n, t_inv)
        n = [nx_[:, :pw] for nx_ in nx]
        t_inv = each(lambda t_, nx_: t_ + nx_[:, pw:], t_inv, nx)
    t_inv = each(lambda n_, t_: t_ + mm(n_, t_), n, t_inv)
    x = each(lambda t_, b_, m_: mm(t_, jnp.concatenate([stack(b_), m_], axis=1)), t_inv, bw, mbr)
    z = [x_[:, :pw] for x_ in x]
    tm_ = [x_[:, pw:] for x_ in x]

    a2 = each(stack, at)
    v2 = each(lambda sl: stack(v_ref[sl]), sls)
    pt = each(lambda z_, a_, t_: mm(z_.T, a_) + jnp.where(eye, jnp.exp(t_), 0.0), z, a2, tot)
    dd = each(lambda k_, z_, w_: mm(k_, z_) + stack(w_), mak, z, kw)
    qt = each(lambda d_, v_: mm(d_.T, v_), dd, v2)
    g = each(lambda r_, t_, a_: stack(r_) + mm(t_.T, a_), rt, tm_, a2)
    hm = each(lambda kr_, k_, t_: kr_ + mm(k_, t_), mkr, mak, tm_)
    hv = each(lambda h_, v_: mm(h_.T, v_), hm, v2)
    st = [st_ref[p] for p in pairs]
    for j in range(n_sub):
        part = slice(j * len(pairs), (j + 1) * len(pairs))
        ys = each(lambda g_, s_, h_: mm(g_, s_) + h_, g[part], st, hv[part])
        st = each(lambda p_, s_, q_: mm(p_, s_) + q_, pt[part], st, qt[part])
        for p in pairs:
            y_ref[sls[part][p]] = ys[p][:ch] + ys[p][ch:]
    for p in pairs:
        st_ref[p] = st[p]


def _rwkv_scan(r, v, kk, lw, kd, bb, n_ctx):
    b, t, w = r.shape
    assert t % SCAN_STEP == 0 and n_ctx % SCAN_STEP == 0 and SCAN_STEP % CHUNK == 0
    nc, ncx = t // SCAN_STEP, n_ctx // SCAN_STEP

    def chunk(d, c):
        return jnp.where(d == 0, c, jnp.where(c < ncx, ncx - 1 - c, nc - 1 + ncx - c))

    shared = pl.BlockSpec((None, SCAN_STEP, w), lambda bb_, d, c: (bb_, chunk(d, c), 0))
    perdir = pl.BlockSpec((None, None, SCAN_STEP, w), lambda bb_, d, c: (bb_, d, chunk(d, c), 0))
    return pl.pallas_call(
        _scan_kernel,
        grid=(b, N_DIRS, nc),
        in_specs=[shared] * 3 + [perdir] * 3,
        out_specs=perdir,
        out_shape=jax.ShapeDtypeStruct((b, N_DIRS, t, w), F32),
        scratch_shapes=[pltpu.VMEM((w // LANES, LANES, LANES), F32)],
        compiler_params=_params(("parallel", "parallel", "arbitrary")),
        name="rwkv_scan",
    )(r, v, kk, lw, kd, bb)


def _out_kernel(y0_ref, y1_ref, bonus_ref, gate_ref, attn_ref, x_ref, lnw_ref, lnb_ref, w_ref, g_ref, o_ref):
    ones = _head_ones(MXU_WIDTH)
    y = y0_ref[...] + y1_ref[...]
    yc = y - _head_sum(y, ones) * (1.0 / HEAD_DIM)
    var = _head_sum(yc * yc, ones) * (1.0 / HEAD_DIM)
    yn = yc * lax.rsqrt(var + GN_EPS) * lnw_ref[...] + lnb_ref[...]
    rw = ((yn + bonus_ref[...]) * gate_ref[...]).astype(BF16)
    mix = jnp.dot(jnp.concatenate([attn_ref[...], rw], axis=1), w_ref[...], preferred_element_type=F32)
    o_ref[...] = x_ref[...] + g_ref[...] * mix


def _mixer_out(y, bonus, gate, attn, xs, ln_w, ln_b, w_out, mod):
    b, t, d = xs.shape
    tm = TOKEN_TILE
    tile = lambda w: pl.BlockSpec((None, tm, w), lambda bb, i: (bb, i, 0))
    ydir = lambda dd: pl.BlockSpec((None, None, tm, RWKV_DIM), lambda bb, i: (bb, dd, i, 0))
    row = lambda w: pl.BlockSpec((1, w), lambda bb, i: (0, 0))
    return pl.pallas_call(
        _out_kernel,
        grid=(b, t // tm),
        in_specs=[ydir(0), ydir(1), tile(RWKV_DIM), tile(RWKV_DIM), tile(ATTN_DIM), tile(d),
                  row(RWKV_DIM), row(RWKV_DIM), pl.BlockSpec(w_out.shape, lambda bb, i: (0, 0)),
                  _mod_spec(b, 2, d)],
        out_specs=tile(d),
        out_shape=jax.ShapeDtypeStruct((b, t, d), F32),
        compiler_params=_params(("parallel", "parallel")),
        name="mixer_out",
    )(y, y, bonus, gate, attn, xs, ln_w, ln_b, w_out, mod)


def _query_kernel(x_ref, gain_ref, sh_ref, sc_ref, w_ref, k1_ref, k2_ref, ht_ref, s1_ref, s2_ref):
    hf = _norm_mod(x_ref[...], gain_ref[...], sh_ref[...], sc_ref[...])
    ht_ref[...] = hf.T.astype(BF16)
    q = jnp.dot(hf.astype(BF16), w_ref[...], preferred_element_type=F32).astype(BF16)
    nt = (((1,), (1,)), ((), ()))
    for hd in range(PEER_HEADS):
        c0 = hd * 2 * PEER_HALF
        s1_ref[hd] = lax.dot_general(k1_ref[...], q[:, c0:c0 + PEER_HALF], nt, preferred_element_type=F32)
        s2_ref[hd] = lax.dot_general(k2_ref[...], q[:, c0 + PEER_HALF:c0 + 2 * PEER_HALF], nt,
                                     preferred_element_type=F32)


def _peer_scores(xs, mod, gain, w_query, keys1, keys2):
    b, t, d = xs.shape
    tm = TOKEN_TILE
    nt = t // tm
    row = lambda w: pl.BlockSpec((1, w), lambda bb, i: (0, 0))
    full = lambda a: pl.BlockSpec(a.shape, lambda bb, i: (0,) * a.ndim)
    sspec = pl.BlockSpec((PEER_HEADS, N_KEYS, tm), lambda bb, i: (0, 0, bb * nt + i))
    sshape = jax.ShapeDtypeStruct((PEER_HEADS, N_KEYS, b * t), F32)
    return pl.pallas_call(
        _query_kernel,
        grid=(b, nt),
        in_specs=[pl.BlockSpec((None, tm, d), lambda bb, i: (bb, i, 0)), row(d), _mod_spec(b, 3, d),
                  _mod_spec(b, 4, d), full(w_query), full(keys1), full(keys2)],
        out_specs=[pl.BlockSpec((d, tm), lambda bb, i: (0, bb * nt + i)), sspec, sspec],
        out_shape=[jax.ShapeDtypeStruct((d, b * t), BF16), sshape, sshape],
        compiler_params=_params(("parallel", "parallel")),
        name="peer_scores",
    )(xs, gain, mod, mod, w_query, keys1, keys2)


def _top_rows(x_ref, out_ref):
    slabs, n, tl = x_ref.shape
    idx = _iota((n, tl), 0).astype(F32)

    def body(i, carry):
        xs = [x_ref[s] for s in range(slabs)]
        ms = [jnp.max(x, axis=0, keepdims=True) for x in xs]
        firsts = [jnp.min(jnp.where(x == m, idx, float(n)), axis=0, keepdims=True) for x, m in zip(xs, ms)]
        for s in range(slabs):
            out_ref[s, pl.ds(i, 1), :] = ms[s]
            x_ref[s] = jnp.where(idx == firsts[s], -jnp.inf, xs[s])
        return carry

    lax.fori_loop(0, _RANKS, body, 0)


_RANKS = PEER_TOPK + 1
_RANK_ROWS = -(-_RANKS // 8) * 8
_CAND_COUNTS = [_RANKS // (i + 1) for i in range(_RANKS)]
_CAND_ROWS = -(-sum(_CAND_COUNTS) // 8) * 8


def _topk_kernel(s1_ref, s2_ref, thr_ref, lse_ref, work_ref, v_ref, cand_ref, top_ref):
    work_ref[:PEER_HEADS] = s1_ref[...]
    work_ref[PEER_HEADS:] = s2_ref[...]
    _top_rows(work_ref, v_ref)
    cand_ref[...] = jnp.full(cand_ref.shape, -jnp.inf, F32)
    for hd in range(PEER_HEADS):
        off = 0
        for i, cnt in enumerate(_CAND_COUNTS):
            cand_ref[hd, off:off + cnt, :] = v_ref[hd, i:i + 1, :] + v_ref[PEER_HEADS + hd, 0:cnt, :]
            off += cnt
    _top_rows(cand_ref, top_ref)
    for hd in range(PEER_HEADS):
        top = top_ref[hd, 0:PEER_TOPK, :]
        mx = top[0:1, :]
        thr_ref[hd:hd + 1, :] = 0.5 * (top[PEER_TOPK - 1:PEER_TOPK, :] + top_ref[hd, PEER_TOPK:_RANKS, :])
        lse_ref[hd:hd + 1, :] = mx + jnp.log(jnp.sum(jnp.exp(top - mx), axis=0, keepdims=True))


def _peer_topk(s1, s2):
    _, _, n = s1.shape
    tl = TOKEN_TILE
    sspec = pl.BlockSpec((PEER_HEADS, N_KEYS, tl), lambda i: (0, 0, i))
    ospec = pl.BlockSpec((PEER_HEADS, tl), lambda i: (0, i))
    oshape = jax.ShapeDtypeStruct((PEER_HEADS, n), F32)
    return pl.pallas_call(
        _topk_kernel,
        grid=(n // tl,),
        in_specs=[sspec, sspec],
        out_specs=[ospec, ospec],
        out_shape=[oshape, oshape],
        scratch_shapes=[pltpu.VMEM((2 * PEER_HEADS, N_KEYS, tl), F32),
                        pltpu.VMEM((2 * PEER_HEADS, _RANK_ROWS, tl), F32),
                        pltpu.VMEM((PEER_HEADS, _CAND_ROWS, tl), F32),
                        pltpu.VMEM((PEER_HEADS, _RANK_ROWS, tl), F32)],
        compiler_params=_params(("parallel",)),
        name="peer_topk",
    )(s1, s2)


def _peer_kernel(ht_ref, s1_ref, s2_ref, thr_ref, lse_ref, u_ref, vt_ref, x_ref, g_ref, o_ref,
                 acc_ref, need_ref, e2_ref, off_ref, *, n_steps):
    ap = pl.program_id(2)

    @pl.when(ap == 0)
    def _():
        acc_ref[...] = jnp.zeros(acc_ref.shape, F32)
        for hd in range(PEER_HEADS):
            s2 = s2_ref[hd]
            m2 = jnp.max(s2, axis=0, keepdims=True)
            need_ref[hd] = thr_ref[hd:hd + 1, :] - s2
            e2_ref[hd] = jnp.exp(s2 - m2).astype(BF16)
            off_ref[hd:hd + 1, :] = m2 - lse_ref[hd:hd + 1, :]

    ht = ht_ref[...]
    sub, rows = EXPERT_SUB, EXPERT_SUB // N_KEYS

    def pre_act(j):
        return jnp.dot(u_ref[j * sub:(j + 1) * sub, :], ht, preferred_element_type=F32)

    pre, upd, part = pre_act(0), None, None
    for j in range(EXPERT_TILE // sub):
        nxt = pre_act(j + 1) if (j + 1) * sub < EXPERT_TILE else None
        gates = []
        for r in range(rows):
            a = ap * (EXPERT_TILE // N_KEYS) + j * rows + r
            w = None
            for hd in range(PEER_HEADS):
                s1a = s1_ref[hd, pl.ds(a, 1), :]
                f = jnp.exp(s1a + off_ref[hd:hd + 1, :]).astype(BF16)
                e = jnp.where(need_ref[hd] <= s1a, e2_ref[hd] * f, jnp.zeros((), BF16))
                w = e if w is None else w + e
            gates.append(w)
        if part is not None:
            upd = part if upd is None else upd + part
        act = 0.5 * pre * (1.0 + lax.erf(pre * (2.0 ** -0.5)))
        gw = jnp.concatenate(gates, axis=0) * act.astype(BF16)
        part = jnp.dot(vt_ref[:, j * sub:(j + 1) * sub], gw, preferred_element_type=F32)
        pre = nxt
    acc_ref[...] += upd + part

    @pl.when(ap == n_steps - 1)
    def _():
        o_ref[...] = x_ref[...] + g_ref[...] * acc_ref[...].T


def _peer_experts(ht, s1, s2, thr, lse, u, vt, xs, mod, latents_only):
    b, t, d = xs.shape
    tm = TOKEN_TILE
    nt = t // tm
    first = 1 if latents_only else 0
    n_steps = u.shape[0] // EXPERT_TILE
    tok = lambda bb, i, a: bb * nt + i + first
    sspec = pl.BlockSpec((PEER_HEADS, N_KEYS, tm), lambda bb, i, a: (0, 0, tok(bb, i, a)))
    rspec = pl.BlockSpec((PEER_HEADS, tm), lambda bb, i, a: (0, tok(bb, i, a)))
    gate = pl.BlockSpec((None, 1, d), lambda bb, i, a: (jnp.where(i + first == 0, b, bb), 0, N_MOD - 1))
    return pl.pallas_call(
        functools.partial(_peer_kernel, n_steps=n_steps),
        grid=(b, nt - first, n_steps),
        in_specs=[pl.BlockSpec((d, tm), lambda bb, i, a: (0, tok(bb, i, a))), sspec, sspec, rspec, rspec,
                  pl.BlockSpec((EXPERT_TILE, d), lambda bb, i, a: (a, 0)),
                  pl.BlockSpec((d, EXPERT_TILE), lambda bb, i, a: (0, a)),
                  pl.BlockSpec((None, tm, d), lambda bb, i, a: (bb, i + first, 0)), gate],
        out_specs=pl.BlockSpec((None, tm, d), lambda bb, i, a: (bb, i, 0)),
        out_shape=jax.ShapeDtypeStruct((b, t - first * tm, d), F32),
        scratch_shapes=[pltpu.VMEM((d, tm), F32), pltpu.VMEM((PEER_HEADS, N_KEYS, tm), F32),
                        pltpu.VMEM((PEER_HEADS, N_KEYS, tm), BF16), pltpu.VMEM((PEER_HEADS, tm), F32)],
        compiler_params=_params(("parallel", "parallel", "arbitrary")),
        name="peer_experts",
    )(ht, s1, s2, thr, lse, u, vt, xs, mod)


def _rope_tables(n_lat, n_ctx):
    pos = jnp.arange(n_lat)
    inv_freq = ROPE_THETA ** (-jnp.arange(AXIS_FREQS, dtype=F32) * 2.0 / AXIS_ROT_DIM)
    ang = jnp.stack([(pos // GRID_W).astype(F32)[:, None] * inv_freq,
                     (pos % GRID_W).astype(F32)[:, None] * inv_freq], axis=1)
    cos = jnp.broadcast_to(jnp.cos(ang)[:, :, None, :], (n_lat, 2, 2, AXIS_FREQS)).reshape(n_lat, HEAD_DIM)
    sin = jnp.stack([-jnp.sin(ang), jnp.sin(ang)], axis=2).reshape(n_lat, HEAD_DIM)
    cos = jnp.concatenate([jnp.ones((n_ctx, HEAD_DIM), F32), cos], axis=0)
    sin = jnp.concatenate([jnp.zeros((n_ctx, HEAD_DIM), F32), sin], axis=0)
    return jnp.tile(cos, (1, LANES // HEAD_DIM)), jnp.tile(sin, (1, LANES // HEAD_DIM))


def _pad_lora(up):
    z = jnp.zeros_like(up[0])
    return jnp.stack([jnp.concatenate([up[0], z], axis=0), jnp.concatenate([z, up[1]], axis=0)]).astype(BF16)


def kernel(x, c, ctx, c_ctx, mod_w, mod_b, norm_mix, norm_ffn, w_in, q_gain, k_gain, shift_taps, decay_base,
           decay_up, iclr_base, iclr_up, gate_up, k_k, k_a, r_k, ln_x_w, ln_x_b, w_out, peer_query, peer_subkeys1,
           peer_subkeys2, expert_u, expert_v):
    b, n_lat, d = x.shape
    n_ctx = ctx.shape[1]
    depth = mod_w.shape[0]
    assert n_ctx == TOKEN_TILE and n_lat % TOKEN_TILE == 0 and b < SUBLANES

    xs = jnp.concatenate([ctx, x], axis=1)
    cc = jnp.concatenate([c, c_ctx[None, :], jnp.zeros((SUBLANES - b - 1, d), F32)], axis=0)
    mod_all = _modulation(cc, mod_w, mod_b)
    cos_t, sin_t = _rope_tables(n_lat, n_ctx)
    row = lambda a: a.reshape(1, -1)

    for l in range(depth):
        mod = mod_all[l].reshape(SUBLANES, 1, N_MOD * d)
        q, k, v, rw = _in_proj(xs, mod, row(norm_mix[l]), w_in[l].astype(BF16),
                               row(jnp.tile(q_gain[l], ATTN_HEADS)), row(jnp.tile(k_gain[l], ATTN_KV_HEADS)),
                               cos_t, sin_t)
        attn = _attention(q, k, v, n_ctx)
        r, vv, kk, gate, bonus, lw, kd, bb = _rwkv_features(
            rw, shift_taps[l], decay_base[l], _pad_lora(decay_up[l]), iclr_base[l], _pad_lora(iclr_up[l]),
            gate_up[l].astype(BF16), row(k_k[l]), row(k_a[l]), row(r_k[l]))
        y = _rwkv_scan(r, vv, kk, lw, kd, bb, n_ctx)
        xs = _mixer_out(y, bonus, gate, attn, xs, row(ln_x_w[l]), row(ln_x_b[l]), w_out[l].astype(BF16), mod)
        ht, s1, s2 = _peer_scores(xs, mod, row(norm_ffn[l]), peer_query[l].astype(BF16),
                                  peer_subkeys1[l].astype(BF16), peer_subkeys2[l].astype(BF16))
        thr, lse = _peer_topk(s1, s2)
        xs = _peer_experts(ht, s1, s2, thr, lse, expert_u[l].astype(BF16), expert_v[l].T.astype(BF16), xs, mod,
                           latents_only=l == depth - 1)
    return xs
```

```python
import functools
import math

import jax
import jax.numpy as jnp
from jax import lax
from jax.experimental import pallas as pl
from jax.experimental.pallas import tpu as pltpu

F32, BF16 = jnp.float32, jnp.bfloat16
HIGHEST = lax.Precision.HIGHEST

HEAD_DIM = 64
ATTN_HEADS = 8
ATTN_KV_HEADS = 2
ATTN_GROUP = ATTN_HEADS // ATTN_KV_HEADS
ATTN_DIM = ATTN_HEADS * HEAD_DIM
KV_DIM = ATTN_KV_HEADS * HEAD_DIM
ATTN_COLS = ATTN_DIM + 2 * KV_DIM
GRID_W = 64
ROPE_THETA = 10000.0
AXIS_ROT_DIM = HEAD_DIM // 2
AXIS_FREQS = AXIS_ROT_DIM // 2
RWKV_DIM = 512
LORA_COLS = 128
N_DIRS = 2
RWKV_COLS = 3 * RWKV_DIM + 3 * LORA_COLS
N_KEYS = 128
PEER_HEADS = 8
PEER_HALF = 128
PEER_TOPK = 16
N_MOD = 6
NORM_EPS = 1e-6
GN_EPS = 64e-5
L2_EPS = 1e-12

LANES = 128
SUBLANES = 8
MXU_WIDTH = 256
HEAD_SHIFT = HEAD_DIM.bit_length() - 1
TOKEN_TILE = 256
KV_TILE = 2816
KEY_SUB = 256
ONES_ROWS = 16
CHUNK = 64
SCAN_STEP = 256
EXPERT_TILE = 4096
EXPERT_SUB = 256
VMEM_LIMIT = 56 * 1024 * 1024


def _params(sem):
    return pltpu.CompilerParams(dimension_semantics=sem, vmem_limit_bytes=VMEM_LIMIT)


def _iota(shape, axis):
    return lax.broadcasted_iota(jnp.int32, shape, axis)


def _head_ones(n):
    return ((_iota((n, n), 0) >> HEAD_SHIFT) == (_iota((n, n), 1) >> HEAD_SHIFT)).astype(BF16)


def _head_sum(x, ones):
    n = ones.shape[0]
    outs = []
    for c0 in range(0, x.shape[1], n):
        rest = x[:, c0:c0 + n]
        acc = None
        for _ in range(3):
            part = rest.astype(BF16)
            rest = rest - part.astype(F32)
            t = jnp.dot(part, ones, preferred_element_type=F32)
            acc = t if acc is None else acc + t
        outs.append(acc)
    return outs[0] if len(outs) == 1 else jnp.concatenate(outs, axis=1)


def _norm_mod(x, gain, shift, scale):
    ms = jnp.mean(x * x, axis=-1, keepdims=True)
    return (x * lax.rsqrt(ms + NORM_EPS) * gain) * (1.0 + scale) + shift


def _mod_kernel(c_ref, w_ref, b_ref, o_ref):
    c = c_ref[...]
    a = c * jax.nn.sigmoid(c)
    o_ref[...] = jnp.dot(a, w_ref[...], preferred_element_type=F32, precision=HIGHEST) + b_ref[...]


def _modulation(cc, mod_w, mod_b):
    depth, d, n = mod_w.shape
    return pl.pallas_call(
        _mod_kernel,
        grid=(depth, n // d),
        in_specs=[pl.BlockSpec((SUBLANES, d), lambda l, j: (0, 0)),
                  pl.BlockSpec((None, d, d), lambda l, j: (l, 0, j)),
                  pl.BlockSpec((None, 1, d), lambda l, j: (l, 0, j))],
        out_specs=pl.BlockSpec((None, SUBLANES, d), lambda l, j: (l, 0, j)),
        out_shape=jax.ShapeDtypeStruct((depth, SUBLANES, n), F32),
        compiler_params=_params(("parallel", "parallel")),
        name="modulation",
    )(cc, mod_w, mod_b.reshape(depth, 1, n))


def _mod_spec(batch, chunk, d):
    return pl.BlockSpec((None, 1, d), lambda b, i, *_: (jnp.where(i == 0, batch, b), 0, chunk))


def _rope(x, cos, sin):
    w = x.shape[1]
    reps = w // LANES
    if reps > 1:
        cos = jnp.concatenate([cos] * reps, axis=1)
        sin = jnp.concatenate([sin] * reps, axis=1)
    first = (_iota(x.shape, 1) & AXIS_FREQS) == 0
    partner = jnp.where(first, pltpu.roll(x, w - AXIS_FREQS, 1), pltpu.roll(x, AXIS_FREQS, 1))
    return x * cos + partner * sin


def _head_norm(x, gain, ones):
    ms = _head_sum(x * x, ones) * (1.0 / HEAD_DIM)
    return x * lax.rsqrt(ms + NORM_EPS) * gain


def _in_kernel(x_ref, gain_ref, sh_ref, sc_ref, w_ref, qg_ref, kg_ref, cos_ref, sin_ref,
               q_ref, k_ref, vt_ref, rw_ref):
    h = _norm_mod(x_ref[...], gain_ref[...], sh_ref[...], sc_ref[...])
    p = jnp.dot(h.astype(BF16), w_ref[...], preferred_element_type=F32)
    cos, sin = cos_ref[...], sin_ref[...]
    q = _head_norm(p[:, :ATTN_DIM], qg_ref[...], _head_ones(MXU_WIDTH))
    q_ref[...] = (_rope(q, cos, sin) * (HEAD_DIM ** -0.5)).astype(BF16)
    k = _head_norm(p[:, ATTN_DIM:ATTN_DIM + KV_DIM], kg_ref[...], _head_ones(KV_DIM))
    k_ref[...] = _rope(k, cos, sin).astype(BF16)
    vt_ref[...] = p[:, ATTN_DIM + KV_DIM:ATTN_COLS].T.astype(BF16)
    rw_ref[...] = p[:, ATTN_COLS:]


def _in_proj(xs, mod, gain, w_in, q_gain, k_gain, cos_t, sin_t):
    b, t, d = xs.shape
    tm = TOKEN_TILE
    n = w_in.shape[1]
    tile = lambda w: pl.BlockSpec((None, tm, w), lambda bb, i: (bb, i, 0))
    row = lambda w: pl.BlockSpec((1, w), lambda bb, i: (0, 0))
    return pl.pallas_call(
        _in_kernel,
        grid=(b, t // tm),
        in_specs=[tile(d), row(d), _mod_spec(b, 0, d), _mod_spec(b, 1, d),
                  pl.BlockSpec((d, n), lambda bb, i: (0, 0)), row(ATTN_DIM), row(KV_DIM),
                  pl.BlockSpec((tm, LANES), lambda bb, i: (i, 0)),
                  pl.BlockSpec((tm, LANES), lambda bb, i: (i, 0))],
        out_specs=[tile(ATTN_DIM), tile(KV_DIM), pl.BlockSpec((None, KV_DIM, tm), lambda bb, i: (bb, 0, i)),
                   tile(RWKV_COLS)],
        out_shape=[jax.ShapeDtypeStruct((b, t, ATTN_DIM), BF16),
                   jax.ShapeDtypeStruct((b, t, KV_DIM), BF16),
                   jax.ShapeDtypeStruct((b, KV_DIM, t), BF16),
                   jax.ShapeDtypeStruct((b, t, RWKV_COLS), F32)],
        compiler_params=_params(("parallel", "parallel")),
        name="in_proj",
    )(xs, gain, mod, mod, w_in, q_gain, k_gain, cos_t, sin_t)


def _attn_step(qt_ref, k_ref, vt_ref, m_ref, acc_ref, n_keys):
    groups = range(ATTN_KV_HEADS)
    ones = jnp.ones((ONES_ROWS, KEY_SUB), BF16)
    m = [m_ref[g] for g in groups]
    acc = [acc_ref[g] for g in groups]

    def scores(c):
        kc = k_ref[c * KEY_SUB:(c + 1) * KEY_SUB, :]
        return [jnp.dot(kc, qt_ref[g], preferred_element_type=F32) for g in groups]

    s = scores(0)
    for c in range(n_keys // KEY_SUB):
        nxt = scores(c + 1) if (c + 1) * KEY_SUB < n_keys else None
        for g in groups:
            m_new = jnp.maximum(m[g], jnp.max(s[g], axis=0, keepdims=True))
            alpha = jnp.exp(m[g] - m_new)
            p = jnp.exp(s[g] - m_new).astype(BF16)
            va = jnp.concatenate(
                [vt_ref[g * HEAD_DIM:(g + 1) * HEAD_DIM, c * KEY_SUB:(c + 1) * KEY_SUB], ones], axis=0)
            acc[g] = alpha * acc[g] + jnp.dot(va, p, preferred_element_type=F32)
            m[g] = m_new
        s = nxt
    for g in groups:
        m_ref[g] = m[g]
        acc_ref[g] = acc[g]


def _attn_kernel(q_ref, k_ref, vt_ref, o_ref, qt_ref, m_ref, acc_ref, *, n_ctx, nk):
    qi, kj = pl.program_id(1), pl.program_id(2)
    gw = ATTN_GROUP * HEAD_DIM

    @pl.when(kj == 0)
    def _():
        qt_ref[...] = jnp.zeros(qt_ref.shape, BF16)
        for g in range(ATTN_KV_HEADS):
            qg_t = q_ref[:, g * gw:(g + 1) * gw].astype(F32).T.astype(BF16)
            for h in range(ATTN_GROUP):
                qt_ref[g, g * HEAD_DIM:(g + 1) * HEAD_DIM, h * qg_t.shape[1]:(h + 1) * qg_t.shape[1]] = (
                    qg_t[h * HEAD_DIM:(h + 1) * HEAD_DIM, :])
        m_ref[...] = jnp.full(m_ref.shape, -jnp.inf, F32)
        acc_ref[...] = jnp.zeros(acc_ref.shape, F32)

    @pl.when((qi == 0) & (kj == 0))
    def _():
        _attn_step(qt_ref, k_ref, vt_ref, m_ref, acc_ref, n_ctx)

    @pl.when(qi > 0)
    def _():
        _attn_step(qt_ref, k_ref, vt_ref, m_ref, acc_ref, k_ref.shape[0])

    @pl.when(kj == nk - 1)
    def _():
        tq = q_ref.shape[0]
        for g in range(ATTN_KV_HEADS):
            o = acc_ref[g, :HEAD_DIM, :] / acc_ref[g, HEAD_DIM:HEAD_DIM + 1, :]
            o = jnp.concatenate([o[:, h * tq:(h + 1) * tq] for h in range(ATTN_GROUP)], axis=0)
            o_ref[:, g * gw:(g + 1) * gw] = o.T.astype(BF16)


def _attention(q, k, vt, n_ctx):
    b, t, _ = q.shape
    tq = TOKEN_TILE
    tk = KV_TILE if t % KV_TILE == 0 else t
    nk = t // tk
    assert n_ctx == tq and n_ctx % KEY_SUB == 0 and tk % KEY_SUB == 0
    kblock = lambda i, j: jnp.where(i == 0, 0, j)
    return pl.pallas_call(
        functools.partial(_attn_kernel, n_ctx=n_ctx, nk=nk),
        grid=(b, t // tq, nk),
        in_specs=[pl.BlockSpec((None, tq, ATTN_DIM), lambda bb, i, j: (bb, i, 0)),
                  pl.BlockSpec((None, tk, KV_DIM), lambda bb, i, j: (bb, kblock(i, j), 0)),
                  pl.BlockSpec((None, KV_DIM, tk), lambda bb, i, j: (bb, 0, kblock(i, j)))],
        out_specs=pl.BlockSpec((None, tq, ATTN_DIM), lambda bb, i, j: (bb, i, 0)),
        out_shape=jax.ShapeDtypeStruct((b, t, ATTN_DIM), BF16),
        scratch_shapes=[pltpu.VMEM((ATTN_KV_HEADS, KV_DIM, ATTN_GROUP * tq), BF16),
                        pltpu.VMEM((ATTN_KV_HEADS, 1, ATTN_GROUP * tq), F32),
                        pltpu.VMEM((ATTN_KV_HEADS, HEAD_DIM + ONES_ROWS, ATTN_GROUP * tq), F32)],
        compiler_params=_params(("parallel", "parallel", "arbitrary")),
        name="attention",
    )(q, k, vt)


def _feat_kernel(rw_ref, prev_ref, next_ref, taps_ref, dbase_ref, dup_ref, ibase_ref, iup_ref, gup_ref,
                 kk_par_ref, ka_ref, rk_ref,
                 r_out, v_out, kk_out, gate_out, bonus_out, lw_out, kd_out, bb_out, *, nt):
    i = pl.program_id(1)
    x = rw_ref[...]
    tm = x.shape[0]
    prev_row = jnp.where(i <= 1, 0.0, prev_ref[SUBLANES - 1:SUBLANES, :])
    next_row = jnp.where((i == 0) | (i == nt - 1), 0.0, next_ref[0:1, :])
    row = _iota(x.shape, 0)
    xm = jnp.where(row == 0, prev_row, pltpu.roll(x, 1, 0))
    xp = jnp.where(row == tm - 1, next_row, pltpu.roll(x, tm - 1, 0))
    xs = xm * taps_ref[0:1, :] + x * taps_ref[1:2, :] + xp * taps_ref[2:3, :]

    o3 = 3 * RWKV_DIM
    r, k, v = xs[:, :RWKV_DIM], xs[:, RWKV_DIM:2 * RWKV_DIM], xs[:, 2 * RWKV_DIM:o3]
    wd = jnp.tanh(xs[:, o3:o3 + LORA_COLS]).astype(BF16)
    ad = xs[:, o3 + LORA_COLS:o3 + 2 * LORA_COLS].astype(BF16)
    gd = jax.nn.sigmoid(xs[:, o3 + 2 * LORA_COLS:]).astype(BF16)
    ones = _head_ones(MXU_WIDTH)
    kkr = k * kk_par_ref[...]
    kk = kkr * lax.rsqrt(_head_sum(kkr * kkr, ones) + L2_EPS)
    r_out[...] = r
    v_out[...] = v
    kk_out[...] = kk
    gate_out[...] = jnp.dot(gd, gup_ref[...], preferred_element_type=F32)
    bonus_out[...] = _head_sum(r * k * rk_ref[...], ones) * v
    for d in range(N_DIRS):
        z = dbase_ref[d:d + 1, :] + jnp.dot(wd, dup_ref[d], preferred_element_type=F32)
        lw_out[d] = jax.nn.sigmoid(z) * (-math.exp(-0.5))
        iclr = jax.nn.sigmoid(ibase_ref[d:d + 1, :] + jnp.dot(ad, iup_ref[d], preferred_element_type=F32))
        kd_out[d] = k * (1.0 + (iclr - 1.0) * ka_ref[...])
        bb_out[d] = kk * iclr


def _rwkv_features(rw, taps, dbase, dup, ibase, iup, gup, k_k, k_a, r_k):
    b, t, w = rw.shape
    tm = TOKEN_TILE
    nt = t // tm
    r8 = tm // SUBLANES
    full = lambda a: pl.BlockSpec(a.shape, lambda bb, i: (0,) * a.ndim)
    tile = pl.BlockSpec((None, tm, RWKV_DIM), lambda bb, i: (bb, i, 0))
    dtile = pl.BlockSpec((None, N_DIRS, tm, RWKV_DIM), lambda bb, i: (bb, 0, i, 0))
    shared = jax.ShapeDtypeStruct((b, t, RWKV_DIM), F32)
    perdir = jax.ShapeDtypeStruct((b, N_DIRS, t, RWKV_DIM), F32)
    consts = (taps, dbase, dup, ibase, iup, gup, k_k, k_a, r_k)
    return pl.pallas_call(
        functools.partial(_feat_kernel, nt=nt),
        grid=(b, nt),
        in_specs=[pl.BlockSpec((None, tm, w), lambda bb, i: (bb, i, 0)),
                  pl.BlockSpec((None, SUBLANES, w), lambda bb, i: (bb, jnp.maximum(i * r8 - 1, 0), 0)),
                  pl.BlockSpec((None, SUBLANES, w),
                               lambda bb, i: (bb, jnp.minimum((i + 1) * r8, nt * r8 - 1), 0))]
                 + [full(a) for a in consts],
        out_specs=[tile] * 5 + [dtile] * 3,
        out_shape=[shared] * 5 + [perdir] * 3,
        compiler_params=_params(("parallel", "parallel")),
        name="rwkv_features",
    )(rw, rw, rw, *consts)


def _scan_kernel(r_ref, v_ref, kk_ref, lw_ref, kd_ref, bb_ref, y_ref, st_ref):
    rev = pl.program_id(1) == 1
    c = pl.program_id(2)

    @pl.when(c == 0)
    def _():
        st_ref[...] = jnp.zeros(st_ref.shape, F32)

    ch, pw = CHUNK, LANES
    ti, si = _iota((ch, ch), 0), _iota((ch, ch), 1)
    tri = (jnp.where(rev, si - ti, ti - si) >= 0).astype(F32)
    a, bcol = _iota((pw, pw), 0), _iota((pw, pw), 1)
    same = (a >> HEAD_SHIFT) == (bcol >> HEAD_SHIFT)
    ahead = jnp.where(rev, a - bcol, bcol - a)
    strict = same & (ahead > 0)
    incl = same & (ahead >= 0)
    eye = a == bcol
    head0 = _iota((ch, pw), 1) < HEAD_DIM

    def stack(x):
        return jnp.concatenate([jnp.where(head0, x, 0.0), jnp.where(head0, 0.0, x)], axis=0)

    def dup(x):
        return jnp.concatenate([x, x], axis=0)

    def mm(x, y):
        return jnp.dot(x.astype(BF16), y.astype(BF16), preferred_element_type=F32)

    pairs = range(RWKV_DIM // pw)
    n_sub = lw_ref.shape[0] // ch
    rows = [pl.ds(pl.multiple_of(jnp.where(rev, n_sub - 1 - j, j) * ch, ch), ch) for j in range(n_sub)]
    sls = [(rows[j], slice(p * pw, (p + 1) * pw)) for j in range(n_sub) for p in pairs]
    each = lambda f, *cols: [f(*args) for args in zip(*cols)]
    lw = [lw_ref[sl] for sl in sls]
    cum = each(lambda a_: jnp.dot(tri, a_, preferred_element_type=F32, precision=HIGHEST), lw)
    tot = each(lambda a_: jnp.sum(a_, axis=0, keepdims=True), lw)
    w_inv = each(lambda c_: jnp.exp(-c_), cum)
    w_rem = each(lambda t_, c_: jnp.exp(t_ - c_), tot, cum)
    at = each(lambda sl, c_, l_: -kk_ref[sl] * jnp.exp(c_ - l_), sls, cum, lw)
    rt = each(lambda sl, c_: r_ref[sl] * jnp.exp(c_), sls, cum)
    bt = each(lambda sl, w_: bb_ref[sl] * w_, sls, w_inv)
    kt = each(lambda sl, w_: kd_ref[sl] * w_, sls, w_inv)
    bw = each(lambda sl, w_: bb_ref[sl] * w_, sls, w_rem)
    kw = each(lambda sl, w_: kd_ref[sl] * w_, sls, w_rem)

    nt_dims = (((1,), (1,)), ((), ()))
    m = each(lambda b_, k_, a_, r_: lax.dot_general(
        jnp.concatenate([stack(b_), stack(k_)], axis=0).astype(BF16),
        jnp.concatenate([dup(a_), dup(r_)], axis=0).astype(BF16), nt_dims, preferred_element_type=F32),
        bt, kt, at, rt)
    mab = each(lambda m_: jnp.where(strict, m_[:pw, :pw], 0.0), m)
    mbr = each(lambda m_: jnp.where(incl, m_[:pw, pw:], 0.0), m)
    mak = each(lambda m_: jnp.where(strict, m_[pw:, :pw], 0.0), m)
    mkr = each(lambda m_: jnp.where(incl, m_[pw:, pw:], 0.0), m)

    levels = CHUNK.bit_length() - 1
    n = each(lambda n_: mm(n_, n_), mab)
    t_inv = each(lambda n_: jnp.where(eye, 1.0, n_), mab)
    for lvl in range(1, levels - 1):
        nx = each(lambda n_, t_: mm(n_, jnp.concatenate([n_, t_], axis=1)), n, t_inv)
        n = [nx_[:, :pw] for nx_ in nx]
        t_inv = each(lambda t_, nx_: t_ + nx_[:, pw:], t_inv, nx)
    t_inv = each(lambda n_, t_: t_ + mm(n_, t_), n, t_inv)
    x = each(lambda t_, b_, m_: mm(t_, jnp.concatenate([stack(b_), m_], axis=1)), t_inv, bw, mbr)
    z = [x_[:, :pw] for x_ in x]
    tm_ = [x_[:, pw:] for x_ in x]

    a2 = each(stack, at)
    v2 = each(lambda sl: stack(v_ref[sl]), sls)
    pt = each(lambda z_, a_, t_: mm(z_.T, a_) + jnp.where(eye, jnp.exp(t_), 0.0), z, a2, tot)
    dd = each(lambda k_, z_, w_: mm(k_, z_) + stack(w_), mak, z, kw)
    qt = each(lambda d_, v_: mm(d_.T, v_), dd, v2)
    g = each(lambda r_, t_, a_: stack(r_) + mm(t_.T, a_), rt, tm_, a2)
    hm = each(lambda kr_, k_, t_: kr_ + mm(k_, t_), mkr, mak, tm_)
    hv = each(lambda h_, v_: mm(h_.T, v_), hm, v2)
    st = [st_ref[p] for p in pairs]
    for j in range(n_sub):
        part = slice(j * len(pairs), (j + 1) * len(pairs))
        ys = each(lambda g_, s_, h_: mm(g_, s_) + h_, g[part], st, hv[part])
        st = each(lambda p_, s_, q_: mm(p_, s_) + q_, pt[part], st, qt[part])
        for p in pairs:
            y_ref[sls[part][p]] = ys[p][:ch] + ys[p][ch:]
    for p in pairs:
        st_ref[p] = st[p]


def _rwkv_scan(r, v, kk, lw, kd, bb, n_ctx):
    b, t, w = r.shape
    assert t % SCAN_STEP == 0 and n_ctx % SCAN_STEP == 0 and SCAN_STEP % CHUNK == 0
    nc, ncx = t // SCAN_STEP, n_ctx // SCAN_STEP

    def chunk(d, c):
        return jnp.where(d == 0, c, jnp.where(c < ncx, ncx - 1 - c, nc - 1 + ncx - c))

    shared = pl.BlockSpec((None, SCAN_STEP, w), lambda bb_, d, c: (bb_, chunk(d, c), 0))
    perdir = pl.BlockSpec((None, None, SCAN_STEP, w), lambda bb_, d, c: (bb_, d, chunk(d, c), 0))
    return pl.pallas_call(
        _scan_kernel,
        grid=(b, N_DIRS, nc),
        in_specs=[shared] * 3 + [perdir] * 3,
        out_specs=perdir,
        out_shape=jax.ShapeDtypeStruct((b, N_DIRS, t, w), F32),
        scratch_shapes=[pltpu.VMEM((w // LANES, LANES, LANES), F32)],
        compiler_params=_params(("parallel", "parallel", "arbitrary")),
        name="rwkv_scan",
    )(r, v, kk, lw, kd, bb)


def _out_kernel(y0_ref, y1_ref, bonus_ref, gate_ref, attn_ref, x_ref, lnw_ref, lnb_ref, w_ref, g_ref, o_ref):
    ones = _head_ones(MXU_WIDTH)
    y = y0_ref[...] + y1_ref[...]
    yc = y - _head_sum(y, ones) * (1.0 / HEAD_DIM)
    var = _head_sum(yc * yc, ones) * (1.0 / HEAD_DIM)
    yn = yc * lax.rsqrt(var + GN_EPS) * lnw_ref[...] + lnb_ref[...]
    rw = ((yn + bonus_ref[...]) * gate_ref[...]).astype(BF16)
    mix = jnp.dot(jnp.concatenate([attn_ref[...], rw], axis=1), w_ref[...], preferred_element_type=F32)
    o_ref[...] = x_ref[...] + g_ref[...] * mix


def _mixer_out(y, bonus, gate, attn, xs, ln_w, ln_b, w_out, mod):
    b, t, d = xs.shape
    tm = TOKEN_TILE
    tile = lambda w: pl.BlockSpec((None, tm, w), lambda bb, i: (bb, i, 0))
    ydir = lambda dd: pl.BlockSpec((None, None, tm, RWKV_DIM), lambda bb, i: (bb, dd, i, 0))
    row = lambda w: pl.BlockSpec((1, w), lambda bb, i: (0, 0))
    return pl.pallas_call(
        _out_kernel,
        grid=(b, t // tm),
        in_specs=[ydir(0), ydir(1), tile(RWKV_DIM), tile(RWKV_DIM), tile(ATTN_DIM), tile(d),
                  row(RWKV_DIM), row(RWKV_DIM), pl.BlockSpec(w_out.shape, lambda bb, i: (0, 0)),
                  _mod_spec(b, 2, d)],
        out_specs=tile(d),
        out_shape=jax.ShapeDtypeStruct((b, t, d), F32),
        compiler_params=_params(("parallel", "parallel")),
        name="mixer_out",
    )(y, y, bonus, gate, attn, xs, ln_w, ln_b, w_out, mod)


def _query_kernel(x_ref, gain_ref, sh_ref, sc_ref, w_ref, k1_ref, k2_ref, ht_ref, s1_ref, s2_ref):
    hf = _norm_mod(x_ref[...], gain_ref[...], sh_ref[...], sc_ref[...])
    ht_ref[...] = hf.T.astype(BF16)
    q = jnp.dot(hf.astype(BF16), w_ref[...], preferred_element_type=F32).astype(BF16)
    nt = (((1,), (1,)), ((), ()))
    for hd in range(PEER_HEADS):
        c0 = hd * 2 * PEER_HALF
        s1_ref[hd] = lax.dot_general(k1_ref[...], q[:, c0:c0 + PEER_HALF], nt, preferred_element_type=F32)
        s2_ref[hd] = lax.dot_general(k2_ref[...], q[:, c0 + PEER_HALF:c0 + 2 * PEER_HALF], nt,
                                     preferred_element_type=F32)


def _peer_scores(xs, mod, gain, w_query, keys1, keys2):
    b, t, d = xs.shape
    tm = TOKEN_TILE
    nt = t // tm
    row = lambda w: pl.BlockSpec((1, w), lambda bb, i: (0, 0))
    full = lambda a: pl.BlockSpec(a.shape, lambda bb, i: (0,) * a.ndim)
    sspec = pl.BlockSpec((PEER_HEADS, N_KEYS, tm), lambda bb, i: (0, 0, bb * nt + i))
    sshape = jax.ShapeDtypeStruct((PEER_HEADS, N_KEYS, b * t), F32)
    return pl.pallas_call(
        _query_kernel,
        grid=(b, nt),
        in_specs=[pl.BlockSpec((None, tm, d), lambda bb, i: (bb, i, 0)), row(d), _mod_spec(b, 3, d),
                  _mod_spec(b, 4, d), full(w_query), full(keys1), full(keys2)],
        out_specs=[pl.BlockSpec((d, tm), lambda bb, i: (0, bb * nt + i)), sspec, sspec],
        out_shape=[jax.ShapeDtypeStruct((d, b * t), BF16), sshape, sshape],
        compiler_params=_params(("parallel", "parallel")),
        name="peer_scores",
    )(xs, gain, mod, mod, w_query, keys1, keys2)


def _top_rows(x_ref, out_ref):
    slabs, n, tl = x_ref.shape
    idx = _iota((n, tl), 0).astype(F32)

    def body(i, carry):
        xs = [x_ref[s] for s in range(slabs)]
        ms = [jnp.max(x, axis=0, keepdims=True) for x in xs]
        firsts = [jnp.min(jnp.where(x == m, idx, float(n)), axis=0, keepdims=True) for x, m in zip(xs, ms)]
        for s in range(slabs):
            out_ref[s, pl.ds(i, 1), :] = ms[s]
            x_ref[s] = jnp.where(idx == firsts[s], -jnp.inf, xs[s])
        return carry

    lax.fori_loop(0, _RANKS, body, 0)


_RANKS = PEER_TOPK + 1
_RANK_ROWS = -(-_RANKS // 8) * 8
_CAND_COUNTS = [_RANKS // (i + 1) for i in range(_RANKS)]
_CAND_ROWS = -(-sum(_CAND_COUNTS) // 8) * 8


def _topk_kernel(s1_ref, s2_ref, thr_ref, lse_ref, work_ref, v_ref, cand_ref, top_ref):
    work_ref[:PEER_HEADS] = s1_ref[...]
    work_ref[PEER_HEADS:] = s2_ref[...]
    _top_rows(work_ref, v_ref)
    cand_ref[...] = jnp.full(cand_ref.shape, -jnp.inf, F32)
    for hd in range(PEER_HEADS):
        off = 0
        for i, cnt in enumerate(_CAND_COUNTS):
            cand_ref[hd, off:off + cnt, :] = v_ref[hd, i:i + 1, :] + v_ref[PEER_HEADS + hd, 0:cnt, :]
            off += cnt
    _top_rows(cand_ref, top_ref)
    for hd in range(PEER_HEADS):
        top = top_ref[hd, 0:PEER_TOPK, :]
        mx = top[0:1, :]
        thr_ref[hd:hd + 1, :] = 0.5 * (top[PEER_TOPK - 1:PEER_TOPK, :] + top_ref[hd, PEER_TOPK:_RANKS, :])
        lse_ref[hd:hd + 1, :] = mx + jnp.log(jnp.sum(jnp.exp(top - mx), axis=0, keepdims=True))


def _peer_topk(s1, s2):
    _, _, n = s1.shape
    tl = TOKEN_TILE
    sspec = pl.BlockSpec((PEER_HEADS, N_KEYS, tl), lambda i: (0, 0, i))
    ospec = pl.BlockSpec((PEER_HEADS, tl), lambda i: (0, i))
    oshape = jax.ShapeDtypeStruct((PEER_HEADS, n), F32)
    return pl.pallas_call(
        _topk_kernel,
        grid=(n // tl,),
        in_specs=[sspec, sspec],
        out_specs=[ospec, ospec],
        out_shape=[oshape, oshape],
        scratch_shapes=[pltpu.VMEM((2 * PEER_HEADS, N_KEYS, tl), F32),
                        pltpu.VMEM((2 * PEER_HEADS, _RANK_ROWS, tl), F32),
                        pltpu.VMEM((PEER_HEADS, _CAND_ROWS, tl), F32),
                        pltpu.VMEM((PEER_HEADS, _RANK_ROWS, tl), F32)],
        compiler_params=_params(("parallel",)),
        name="peer_topk",
    )(s1, s2)


def _peer_kernel(ht_ref, s1_ref, s2_ref, thr_ref, lse_ref, u_ref, vt_ref, x_ref, g_ref, o_ref,
                 acc_ref, need_ref, e2_ref, off_ref, *, n_steps):
    ap = pl.program_id(2)

    @pl.when(ap == 0)
    def _():
        acc_ref[...] = jnp.zeros(acc_ref.shape, F32)
        for hd in range(PEER_HEADS):
            s2 = s2_ref[hd]
            m2 = jnp.max(s2, axis=0, keepdims=True)
            need_ref[hd] = thr_ref[hd:hd + 1, :] - s2
            e2_ref[hd] = jnp.exp(s2 - m2).astype(BF16)
            off_ref[hd:hd + 1, :] = m2 - lse_ref[hd:hd + 1, :]

    ht = ht_ref[...]
    sub, rows = EXPERT_SUB, EXPERT_SUB // N_KEYS

    def pre_act(j):
        return jnp.dot(u_ref[j * sub:(j + 1) * sub, :], ht, preferred_element_type=F32)

    pre, upd, part = pre_act(0), None, None
    for j in range(EXPERT_TILE // sub):
        nxt = pre_act(j + 1) if (j + 1) * sub < EXPERT_TILE else None
        gates = []
        for r in range(rows):
            a = ap * (EXPERT_TILE // N_KEYS) + j * rows + r
            w = None
            for hd in range(PEER_HEADS):
                s1a = s1_ref[hd, pl.ds(a, 1), :]
                f = jnp.exp(s1a + off_ref[hd:hd + 1, :]).astype(BF16)
                e = jnp.where(need_ref[hd] <= s1a, e2_ref[hd] * f, jnp.zeros((), BF16))
                w = e if w is None else w + e
            gates.append(w)
        if part is not None:
            upd = part if upd is None else upd + part
        act = 0.5 * pre * (1.0 + lax.erf(pre * (2.0 ** -0.5)))
        gw = jnp.concatenate(gates, axis=0) * act.astype(BF16)
        part = jnp.dot(vt_ref[:, j * sub:(j + 1) * sub], gw, preferred_element_type=F32)
        pre = nxt
    acc_ref[...] += upd + part

    @pl.when(ap == n_steps - 1)
    def _():
        o_ref[...] = x_ref[...] + g_ref[...] * acc_ref[...].T


def _peer_experts(ht, s1, s2, thr, lse, u, vt, xs, mod, latents_only):
    b, t, d = xs.shape
    tm = TOKEN_TILE
    nt = t // tm
    first = 1 if latents_only else 0
    n_steps = u.shape[0] // EXPERT_TILE
    tok = lambda bb, i, a: bb * nt + i + first
    sspec = pl.BlockSpec((PEER_HEADS, N_KEYS, tm), lambda bb, i, a: (0, 0, tok(bb, i, a)))
    rspec = pl.BlockSpec((PEER_HEADS, tm), lambda bb, i, a: (0, tok(bb, i, a)))
    gate = pl.BlockSpec((None, 1, d), lambda bb, i, a: (jnp.where(i + first == 0, b, bb), 0, N_MOD - 1))
    return pl.pallas_call(
        functools.partial(_peer_kernel, n_steps=n_steps),
        grid=(b, nt - first, n_steps),
        in_specs=[pl.BlockSpec((d, tm), lambda bb, i, a: (0, tok(bb, i, a))), sspec, sspec, rspec, rspec,
                  pl.BlockSpec((EXPERT_TILE, d), lambda bb, i, a: (a, 0)),
                  pl.BlockSpec((d, EXPERT_TILE), lambda bb, i, a: (0, a)),
                  pl.BlockSpec((None, tm, d), lambda bb, i, a: (bb, i + first, 0)), gate],
        out_specs=pl.BlockSpec((None, tm, d), lambda bb, i, a: (bb, i, 0)),
        out_shape=jax.ShapeDtypeStruct((b, t - first * tm, d), F32),
        scratch_shapes=[pltpu.VMEM((d, tm), F32), pltpu.VMEM((PEER_HEADS, N_KEYS, tm), F32),
                        pltpu.VMEM((PEER_HEADS, N_KEYS, tm), BF16), pltpu.VMEM((PEER_HEADS, tm), F32)],
        compiler_params=_params(("parallel", "parallel", "arbitrary")),
        name="peer_experts",
    )(ht, s1, s2, thr, lse, u, vt, xs, mod)


def _rope_tables(n_lat, n_ctx):
    pos = jnp.arange(n_lat)
    inv_freq = ROPE_THETA ** (-jnp.arange(AXIS_FREQS, dtype=F32) * 2.0 / AXIS_ROT_DIM)
    ang = jnp.stack([(pos // GRID_W).astype(F32)[:, None] * inv_freq,
                     (pos % GRID_W).astype(F32)[:, None] * inv_freq], axis=1)
    cos = jnp.broadcast_to(jnp.cos(ang)[:, :, None, :], (n_lat, 2, 2, AXIS_FREQS)).reshape(n_lat, HEAD_DIM)
    sin = jnp.stack([-jnp.sin(ang), jnp.sin(ang)], axis=2).reshape(n_lat, HEAD_DIM)
    cos = jnp.concatenate([jnp.ones((n_ctx, HEAD_DIM), F32), cos], axis=0)
    sin = jnp.concatenate([jnp.zeros((n_ctx, HEAD_DIM), F32), sin], axis=0)
    return jnp.tile(cos, (1, LANES // HEAD_DIM)), jnp.tile(sin, (1, LANES // HEAD_DIM))


def _pad_lora(up):
    z = jnp.zeros_like(up[0])
    return jnp.stack([jnp.concatenate([up[0], z], axis=0), jnp.concatenate([z, up[1]], axis=0)]).astype(BF16)


def kernel(x, c, ctx, c_ctx, mod_w, mod_b, norm_mix, norm_ffn, w_in, q_gain, k_gain, shift_taps, decay_base,
           decay_up, iclr_base, iclr_up, gate_up, k_k, k_a, r_k, ln_x_w, ln_x_b, w_out, peer_query, peer_subkeys1,
           peer_subkeys2, expert_u, expert_v):
    b, n_lat, d = x.shape
    n_ctx = ctx.shape[1]
    depth = mod_w.shape[0]
    assert n_ctx == TOKEN_TILE and n_lat % TOKEN_TILE == 0 and b < SUBLANES

    xs = jnp.concatenate([ctx, x], axis=1)
    cc = jnp.concatenate([c, c_ctx[None, :], jnp.zeros((SUBLANES - b - 1, d), F32)], axis=0)
    mod_all = _modulation(cc, mod_w, mod_b)
    cos_t, sin_t = _rope_tables(n_lat, n_ctx)
    row = lambda a: a.reshape(1, -1)

    for l in range(depth):
        mod = mod_all[l].reshape(SUBLANES, 1, N_MOD * d)
        q, k, v, rw = _in_proj(xs, mod, row(norm_mix[l]), w_in[l].astype(BF16),
                               row(jnp.tile(q_gain[l], ATTN_HEADS)), row(jnp.tile(k_gain[l], ATTN_KV_HEADS)),
                               cos_t, sin_t)
        attn = _attention(q, k, v, n_ctx)
        r, vv, kk, gate, bonus, lw, kd, bb = _rwkv_features(
            rw, shift_taps[l], decay_base[l], _pad_lora(decay_up[l]), iclr_base[l], _pad_lora(iclr_up[l]),
            gate_up[l].astype(BF16), row(k_k[l]), row(k_a[l]), row(r_k[l]))
        y = _rwkv_scan(r, vv, kk, lw, kd, bb, n_ctx)
        xs = _mixer_out(y, bonus, gate, attn, xs, row(ln_x_w[l]), row(ln_x_b[l]), w_out[l].astype(BF16), mod)
        ht, s1, s2 = _peer_scores(xs, mod, row(norm_ffn[l]), peer_query[l].astype(BF16),
                                  peer_subkeys1[l].astype(BF16), peer_subkeys2[l].astype(BF16))
        thr, lse = _peer_topk(s1, s2)
        xs = _peer_experts(ht, s1, s2, thr, lse, expert_u[l].astype(BF16), expert_v[l].T.astype(BF16), xs, mod,
                           latents_only=l == depth - 1)
    return xs
```
